```python
import jax, jax.numpy as jnp
from jax import lax
import numpy as np

D_MODEL = 1024
BATCH = 8
SEQ = 2048
DEPTH = 4
DEC_BATCH = 128
DEC_SEQ = 1
PAST_LEN = 16384
PAGE_SIZE = 128

D_A = D_MODEL // 2
H_A = 4
DK_A = D_A // H_A
DV_A = D_A // H_A
D_B = D_MODEL // 2
NB_B = 4
BS_B = D_B // NB_B
CONV_W = 4
RG_C = 8.0
H_C = 4
D_C = D_MODEL // 2
DK_C_TOT = D_MODEL // 4
DK_C = DK_C_TOT // H_C
DV_C = D_C // H_C
GATE_RANK = 16
GLA_TAU = 16.0
CHUNK = 64
FFN_DIM = 2816
N_SUB = 3
EPS = 1e-6
IN_SPLITS = (D_A, D_A, D_A, D_A, D_B, D_B, DK_C_TOT, DK_C_TOT, D_C, D_C, GATE_RANK, D_MODEL, D_MODEL, D_MODEL)
N_IN = sum(IN_SPLITS)

kernel_name = 'hybrid_hgrn2_rglru_gla_macaron_adaln_step'


def rmsnorm(x, g):
    xf = x.astype(jnp.float32)
    y = xf * lax.rsqrt(jnp.mean(xf * xf, axis=-1, keepdims=True) + EPS)
    return (y * g.astype(jnp.float32)).astype(x.dtype)


def ada_norm(x, g, shift, scale):
    return rmsnorm(x, g) * (1 + scale) + shift


def swiglu(h, w_i, w_o):
    gt, up = jnp.split(h @ w_i, 2, axis=-1)
    return (jax.nn.silu(gt) * up) @ w_o


def chunked_gated_recurrence(q, k, v, log_f, s0):
    B, T, H, DK = q.shape
    DV = v.shape[-1]
    C = min(CHUNK, T)
    n = -(-T // C)
    pad = n * C - T
    f32 = jnp.float32

    def to_chunks(a):
        a = a.astype(f32)
        if pad:
            a = jnp.pad(a, ((0, 0), (0, pad), (0, 0), (0, 0)))
        return a.reshape(B, n, C, H, a.shape[-1]).transpose(1, 0, 3, 2, 4)

    qc, kc, vc, gc = to_chunks(q), to_chunks(k), to_chunks(v), to_chunks(log_f)
    causal = jnp.tril(jnp.ones((C, C), dtype=bool))[:, :, None]

    def step(S, inp):
        qb, kb, vb, gb = inp
        b = jnp.cumsum(gb, axis=2)
        inter = jnp.einsum('bhtd,bhdv->bhtv', qb * jnp.exp(b), S)
        diff = b[:, :, :, None, :] - b[:, :, None, :, :]
        decay = jnp.exp(jnp.where(causal, diff, -jnp.inf))
        scores = jnp.einsum('bhtd,bhsd,bhtsd->bhts', qb, kb, decay)
        intra = jnp.einsum('bhts,bhsv->bhtv', scores, vb)
        b_end = b[:, :, -1:, :]
        S_new = jnp.exp(b_end[:, :, 0, :])[..., None] * S + jnp.einsum(
            'bhsd,bhsv->bhdv', kb * jnp.exp(b_end - b), vb)
        return S_new, inter + intra

    S_fin, o = lax.scan(step, s0.astype(f32), (qc, kc, vc, gc))
    o = o.transpose(1, 0, 3, 2, 4).reshape(B, n * C, H, DV)[:, :T]
    return o.astype(v.dtype), S_fin.astype(s0.dtype)


def head_out(o, gain, g):
    B, T, H, DV = o.shape
    return rmsnorm(o, gain).reshape(B, T, H * DV) * jax.nn.silu(g)


def rglru_branch(xb, gb, buf, h0, conv_w, conv_b, wa, ba, wx, bx, lam):
    B, T, _ = xb.shape
    f32 = jnp.float32
    xpad = jnp.concatenate([buf.astype(xb.dtype), xb], axis=1)
    conv = conv_b
    for j in range(CONV_W):
        conv = conv + xpad[:, j:j + T] * conv_w[j]
    new_buf = xpad[:, T:]
    xr = conv.reshape(B, T, NB_B, BS_B)
    r = jax.nn.sigmoid((jnp.einsum('btnk,nkj->btnj', xr, wa).reshape(B, T, D_B) + ba).astype(f32))
    i = jax.nn.sigmoid((jnp.einsum('btnk,nkj->btnj', xr, wx).reshape(B, T, D_B) + bx).astype(f32))
    log_a = -RG_C * r * jax.nn.softplus(-lam.astype(f32))
    a = jnp.exp(log_a)
    u = jnp.sqrt(-jnp.expm1(2.0 * log_a)) * i * conv.astype(f32)

    def combine(lft, rgt):
        a1, b1 = lft
        a2, b2 = rgt
        return a1 * a2, a2 * b1 + b2

    a_cum, u_cum = lax.associative_scan(combine, (a, u), axis=1)
    h = u_cum + a_cum * h0.astype(f32)[:, None]
    y = jax.nn.gelu(gb) * h.astype(gb.dtype)
    return y, new_buf, h[:, -1].astype(h0.dtype)


def mixer_block(h, lb, st_a, st_cv, st_h, st_c, w_in, hgrn_norm, conv_w, conv_b, rg_wa, rg_ba,
                rg_wx, rg_bx, rg_lam, gla_wa, gla_ba, gla_norm, w_br_a, w_br_b, w_br_c, w_out):
    B, T, _ = h.shape
    f32 = jnp.float32
    idx = np.cumsum(IN_SPLITS)[:-1].tolist()
    (a_q, a_f, a_i, a_g, b_x, b_g, c_q, c_k, c_v, c_g, c_a,
     m_a, m_b, m_c) = jnp.split(h @ w_in, idx, axis=-1)

    fz = a_f.astype(f32)
    log_f_a = jnp.logaddexp(jnp.log(lb), jnp.log1p(-lb) + jax.nn.log_sigmoid(fz))
    k_a = (1 - lb) * jax.nn.sigmoid(-fz)
    o_a, s_a = chunked_gated_recurrence(
        jax.nn.silu(a_q).reshape(B, T, H_A, DK_A), k_a.reshape(B, T, H_A, DK_A),
        a_i.reshape(B, T, H_A, DV_A), log_f_a.reshape(B, T, H_A, DK_A), st_a)
    y_a = head_out(o_a, hgrn_norm, a_g)

    y_b, new_cv, new_h = rglru_branch(b_x, b_g, st_cv, st_h, conv_w, conv_b,
                                      rg_wa, rg_ba, rg_wx, rg_bx, rg_lam)

    log_f_c = jax.nn.log_sigmoid((c_a @ gla_wa + gla_ba).astype(f32)) / GLA_TAU
    o_c, s_c = chunked_gated_recurrence(
        c_q.reshape(B, T, H_C, DK_C) * (DK_C ** -0.5), c_k.reshape(B, T, H_C, DK_C),
        c_v.reshape(B, T, H_C, DV_C), log_f_c.reshape(B, T, H_C, DK_C), st_c)
    y_c = head_out(o_c, gla_norm, c_g)

    merged = (jax.nn.sigmoid(m_a) * (y_a @ w_br_a) + jax.nn.sigmoid(m_b) * (y_b @ w_br_b)
              + jax.nn.sigmoid(m_c) * (y_c @ w_br_c))
    return merged @ w_out, s_a, new_cv, new_h, s_c


def setup_inputs(seed: int = 0) -> dict:
    key = jax.random.key(seed)
    ks = iter(jax.random.split(key, 40))
    nrm = lambda shape, s: s * jax.random.normal(next(ks), shape, jnp.float32)
    a0 = jax.random.uniform(next(ks), (DEPTH, D_B), jnp.float32, minval=0.9, maxval=0.999)
    return {
        'x_prompt': nrm((BATCH, SEQ, D_MODEL), 1.0),
        'x_sample': nrm((DEC_BATCH, DEC_SEQ, D_MODEL), 1.0),
        'c_prompt': nrm((BATCH, D_MODEL), 1.0),
        'c_sample': nrm((DEC_BATCH, D_MODEL), 1.0),
        'state_hgrn': nrm((DEPTH, DEC_BATCH, H_A, DK_A, DV_A), 1.0),
        'state_conv': nrm((DEPTH, DEC_BATCH, CONV_W - 1, D_B), 1.0),
        'state_rglru': nrm((DEPTH, DEC_BATCH, D_B), 0.5),
        'state_gla': nrm((DEPTH, DEC_BATCH, H_C, DK_C, DV_C), 1.0),
        'w_ada': nrm((DEPTH, D_MODEL, N_SUB * 3 * D_MODEL), D_MODEL ** -0.5),
        'b_ada': nrm((DEPTH, N_SUB * 3 * D_MODEL), 0.02),
        'norm_gain': 1.0 + nrm((DEPTH, N_SUB, D_MODEL), 0.02),
        'w_ffn_in': nrm((DEPTH, 2, D_MODEL, 2 * FFN_DIM), D_MODEL ** -0.5),
        'w_ffn_out': nrm((DEPTH, 2, FFN_DIM, D_MODEL), FFN_DIM ** -0.5),
        'w_in': nrm((DEPTH, D_MODEL, N_IN), D_MODEL ** -0.5),
        'hgrn_lb_logits': nrm((DEPTH, D_A), 0.1),
        'hgrn_norm': 1.0 + nrm((DEPTH, DV_A), 0.02),
        'conv_w': nrm((DEPTH, CONV_W, D_B), CONV_W ** -0.5),
        'conv_b': nrm((DEPTH, D_B), 0.02),
        'rg_wa': nrm((DEPTH, NB_B, BS_B, BS_B), BS_B ** -0.5),
        'rg_ba': nrm((DEPTH, D_B), 0.02),
        'rg_wx': nrm((DEPTH, NB_B, BS_B, BS_B), BS_B ** -0.5),
        'rg_bx': nrm((DEPTH, D_B), 0.02),
        'rg_lambda': jnp.log(a0) - jnp.log1p(-a0),
        'gla_w_alpha': nrm((DEPTH, GATE_RANK, DK_C_TOT), GATE_RANK ** -0.5),
        'gla_b_alpha': nrm((DEPTH, DK_C_TOT), 0.02),
        'gla_norm': 1.0 + nrm((DEPTH, DV_C), 0.02),
        'w_br_a': nrm((DEPTH, D_A, D_MODEL), D_A ** -0.5),
        'w_br_b': nrm((DEPTH, D_B, D_MODEL), D_B ** -0.5),
        'w_br_c': nrm((DEPTH, D_C, D_MODEL), D_C ** -0.5),
        'w_out': nrm((DEPTH, D_MODEL, D_MODEL), D_MODEL ** -0.5),
        'final_norm': 1.0 + nrm((D_MODEL,), 0.02),
    }


def reference(x_prompt, x_sample, c_prompt, c_sample, state_hgrn, state_conv, state_rglru, state_gla,
              w_ada, b_ada, norm_gain, w_ffn_in, w_ffn_out, w_in, hgrn_lb_logits, hgrn_norm,
              conv_w, conv_b, rg_wa, rg_ba, rg_wx, rg_bx, rg_lambda, gla_w_alpha, gla_b_alpha,
              gla_norm, w_br_a, w_br_b, w_br_c, w_out, final_norm):
    lb_all = jnp.cumsum(jax.nn.softmax(hgrn_lb_logits.astype(jnp.float32), axis=0), axis=0)
    lb_all = lb_all - lb_all[0]

    def run(x, c, s_a, s_cv, s_h, s_c):
        B = x.shape[0]
        na, ncv, nh, nc = [], [], [], []
        for l in range(DEPTH):
            mod = (c @ w_ada[l] + b_ada[l]).reshape(B, N_SUB, 3, D_MODEL)[:, :, :, None, :]
            h = ada_norm(x, norm_gain[l, 0], mod[:, 0, 0], mod[:, 0, 1])
            x = x + 0.5 * mod[:, 0, 2] * swiglu(h, w_ffn_in[l, 0], w_ffn_out[l, 0])
            h = ada_norm(x, norm_gain[l, 1], mod[:, 1, 0], mod[:, 1, 1])
            y, sa, scv, sh, sc = mixer_block(
                h, lb_all[l], s_a[l], s_cv[l], s_h[l], s_c[l], w_in[l], hgrn_norm[l],
                conv_w[l], conv_b[l], rg_wa[l], rg_ba[l], rg_wx[l], rg_bx[l], rg_lambda[l],
                gla_w_alpha[l], gla_b_alpha[l], gla_norm[l], w_br_a[l], w_br_b[l], w_br_c[l], w_out[l])
            x = x + mod[:, 1, 2] * y
            h = ada_norm(x, norm_gain[l, 2], mod[:, 2, 0], mod[:, 2, 1])
            x = x + 0.5 * mod[:, 2, 2] * swiglu(h, w_ffn_in[l, 1], w_ffn_out[l, 1])
            na.append(sa); ncv.append(scv); nh.append(sh); nc.append(sc)
        return (rmsnorm(x, final_norm), jnp.stack(na), jnp.stack(ncv), jnp.stack(nh), jnp.stack(nc))

    Bp = x_prompt.shape[0]
    dt = x_prompt.dtype
    y_prompt, hgrn_p, conv_p, rglru_p, gla_p = run(
        x_prompt, c_prompt,
        jnp.zeros((DEPTH, Bp, H_A, DK_A, DV_A), dt),
        jnp.zeros((DEPTH, Bp, CONV_W - 1, D_B), dt),
        jnp.zeros((DEPTH, Bp, D_B), dt),
        jnp.zeros((DEPTH, Bp, H_C, DK_C, DV_C), dt))
    y_sample, hgrn_s, conv_s, rglru_s, gla_s = run(
        x_sample, c_sample, state_hgrn, state_conv, state_rglru, state_gla)
    return (y_prompt, y_sample, hgrn_p, hgrn_s, conv_p, conv_s, rglru_p, rglru_s, gla_p, gla_s)
```

```python
import functools
import math

import jax
import jax.numpy as jnp
from jax import lax
from jax.experimental import pallas as pl
from jax.experimental.pallas import tpu as pltpu

F32 = jnp.float32
BF16 = jnp.bfloat16

EPS = 1e-6
H_A = 4
NB_B = 4
CONV_W = 4
RG_C = 8.0
H_C = 4
GATE_RANK = 16
GLA_TAU = 16.0
N_SUB = 3

LANES = 128
SUBLANES = 8
VMEM_LIMIT_BYTES = 56 * 1024 * 1024

REC_CHUNK = 128
MIX_TILE = 256
FFN_TILE = 512
DEC_GROUP = 8


def _dot(a, b):
    return jnp.dot(a, b, preferred_element_type=F32)


def _dot_nt(a, b):
    return lax.dot_general(a, b, (((1,), (1,)), ((), ())), preferred_element_type=F32)


def _dot_tn(a, b):
    return lax.dot_general(a, b, (((0,), (0,)), ((), ())), preferred_element_type=F32)


def _sigmoid(x):
    return 1.0 / (1.0 + jnp.exp(-x))


def _silu(x):
    return x * _sigmoid(x)


def _softplus(x):
    return jnp.maximum(x, 0.0) + jnp.log1p(jnp.exp(-jnp.abs(x)))


def _log_sigmoid(x):
    return -_softplus(-x)


def _gelu_tanh(x):
    c = math.sqrt(2.0 / math.pi)
    return x * (0.5 * (1.0 + jnp.tanh(c * (x + 0.044715 * (x * x * x)))))


def _rms(x, gain):
    ms = jnp.mean(x * x, axis=-1, keepdims=True)
    return x * lax.rsqrt(ms + EPS) * gain


def _ada_norm(x, gain, shift, scale):
    return _rms(x, gain) * (1.0 + scale) + shift


def _ada_kernel(c_ref, w_ref, b_ref, o_ref):
    o_ref[...] = _dot(c_ref[...].astype(BF16), w_ref[...].astype(BF16)) + b_ref[...]


def _ada_call(c_all, w_ada, b_ada):
    depth, d, nm = w_ada.shape
    rows = c_all.shape[0]
    tn = 1536
    assert nm % tn == 0
    return pl.pallas_call(
        _ada_kernel,
        grid=(depth, nm // tn),
        in_specs=[
            pl.BlockSpec((rows, d), lambda l, n: (0, 0)),
            pl.BlockSpec((None, d, tn), lambda l, n: (l, 0, n)),
            pl.BlockSpec((None, 1, tn), lambda l, n: (l, 0, n)),
        ],
        out_specs=pl.BlockSpec((None, rows, tn), lambda l, n: (l, 0, n)),
        out_shape=jax.ShapeDtypeStruct((depth, rows, nm), F32),
        compiler_params=pltpu.CompilerParams(
            dimension_semantics=("arbitrary", "arbitrary"),
            vmem_limit_bytes=VMEM_LIMIT_BYTES),
        name="ada_mod",
    )(c_all, w_ada, b_ada.reshape(depth, 1, nm))


def _ffn_kernel(x_ref, sh_ref, sc_ref, gt_ref, gain_ref, wi_ref, wo_ref, *rest, n_chunks, final):
    if final:
        fg_ref, o_ref = rest
    else:
        (o_ref,) = rest
    x = x_ref[...]
    h = _ada_norm(x, gain_ref[...], sh_ref[0], sc_ref[0]).astype(BF16)
    f = wo_ref.shape[0]
    fc = f // n_chunks
    acc = None
    for c in range(n_chunks):
        g = _dot(h, wi_ref[:, c * fc:(c + 1) * fc])
        u = _dot(h, wi_ref[:, f + c * fc:f + (c + 1) * fc])
        a = (_silu(g) * u).astype(BF16)
        part = _dot(a, wo_ref[c * fc:(c + 1) * fc, :])
        acc = part if acc is None else acc + part
    y = x + (0.5 * gt_ref[0]) * acc
    if final:
        y = _rms(y, fg_ref[...])
    o_ref[...] = y


def _mod_spec(rows_per_group, d, layer, col, group_of):
    return pl.BlockSpec((None, 1, rows_per_group, d), lambda i: (layer, group_of(i), 0, col))


def _ffn_call(x, mod4, gain, wi, wo, layer, which, sub, tile, tiles_per_group, final_gain=None):
    n, d = x.shape
    f = wo.shape[2]
    r = mod4.shape[2]
    group_of = (lambda i: i // tiles_per_group)
    final = final_gain is not None
    in_specs = [
        pl.BlockSpec((tile, d), lambda i: (i, 0)),
        _mod_spec(r, d, layer, 3 * sub + 0, group_of),
        _mod_spec(r, d, layer, 3 * sub + 1, group_of),
        _mod_spec(r, d, layer, 3 * sub + 2, group_of),
        pl.BlockSpec((None, None, 1, d), lambda i: (layer, sub, 0, 0)),
        pl.BlockSpec((None, None, d, 2 * f), lambda i: (layer, which, 0, 0),
                     pipeline_mode=pl.Buffered(1)),
        pl.BlockSpec((None, None, f, d), lambda i: (layer, which, 0, 0),
                     pipeline_mode=pl.Buffered(1)),
    ]
    args = [x, mod4, mod4, mod4, gain, wi, wo]
    if final:
        in_specs.append(pl.BlockSpec((1, d), lambda i: (0, 0)))
        args.append(final_gain.reshape(1, d))
    return pl.pallas_call(
        functools.partial(_ffn_kernel, n_chunks=2, final=final),
        grid=(n // tile,),
        in_specs=in_specs,
        out_specs=pl.BlockSpec((tile, d), lambda i: (i, 0)),
        out_shape=jax.ShapeDtypeStruct((n, d), F32),
        compiler_params=pltpu.CompilerParams(
            dimension_semantics=("arbitrary",), vmem_limit_bytes=VMEM_LIMIT_BYTES),
        name="ffn",
    )(*args)


def _hgrn_gates(a_q, a_f, loglb, log1mlb, omlb):
    q = _silu(a_q)
    c = log1mlb + _log_sigmoid(a_f)
    m = jnp.maximum(loglb, c)
    log_f = m + jnp.log1p(jnp.exp(-jnp.abs(loglb - c)))
    k = omlb * _sigmoid(-a_f)
    return q, k, log_f


def _gla_log_f(c_a, wal_ref, bal):
    return _log_sigmoid(_dot(c_a.astype(BF16), wal_ref[...]) + bal) * (1.0 / GLA_TAU)


def _rglru_gates(conv, wa_ref, wx_ref, ba, bx, lam):
    cb = conv.astype(BF16)
    bs = conv.shape[1] // NB_B
    r = jnp.concatenate([_dot(cb[:, n * bs:(n + 1) * bs], wa_ref[n]) for n in range(NB_B)], axis=-1)
    i = jnp.concatenate([_dot(cb[:, n * bs:(n + 1) * bs], wx_ref[n]) for n in range(NB_B)], axis=-1)
    r = _sigmoid(r + ba)
    i = _sigmoid(i + bx)
    log_a = (-RG_C) * r * _softplus(-lam)
    a = jnp.exp(log_a)
    u = jnp.sqrt(1.0 - jnp.exp(2.0 * log_a)) * i * conv
    return a, u


def _head_out(o_heads, gain, gate):
    return jnp.concatenate([_rms(o, gain) for o in o_heads], axis=-1) * _silu(gate)


def _merge(x, gate, zm, y_a, y_b, y_c, wbr_ref, wout_ref):
    d = x.shape[1]
    merged = (_sigmoid(zm[:, 0:d]) * _dot(y_a.astype(BF16), wbr_ref[0])
              + _sigmoid(zm[:, d:2 * d]) * _dot(y_b.astype(BF16), wbr_ref[1])
              + _sigmoid(zm[:, 2 * d:3 * d]) * _dot(y_c.astype(BF16), wbr_ref[2]))
    return x + gate * _dot(merged.astype(BF16), wout_ref[...])


def _level_operands(q, k, g, chunk):
    n, w = g.shape
    row = lax.broadcasted_iota(jnp.int32, (n, w), 0)
    rin = row & (chunk - 1)
    b = g
    s = 1
    while s < chunk:
        b = b + jnp.where(rin >= s, pltpu.roll(b, s, 0), 0.0)
        s *= 2
    xs = []
    n_levels = chunk.bit_length() - 1
    for l in range(n_levels):
        m = 1 << l
        right = (rin & m) != 0
        if l == 0:
            dlt = jnp.where(right, g, 0.0)
        elif l == 1:
            gm1 = pltpu.roll(g, 1, 0)
            gp1 = pltpu.roll(g, n - 1, 0)
            r4 = rin & 3
            dlt = jnp.where(r4 == 0, gp1, jnp.where(r4 == 1, 0.0, jnp.where(r4 == 2, g, g + gm1)))
        else:
            p = 2 * m
            b3 = b.reshape(n // p, p, w)
            ref = jnp.broadcast_to(b3[:, m - 1:m, :], (n // p, p, w)).reshape(n, w)
            dlt = jnp.where(right, b - ref, ref - b)
        xs.append((jnp.where(right, q, k) * jnp.exp(dlt)).astype(BF16))
    b3 = b.reshape(n // chunk, chunk, w)
    bend3 = b3[:, chunk - 1:chunk, :]
    bend = jnp.broadcast_to(bend3, (n // chunk, chunk, w)).reshape(n, w)
    qe = (q * jnp.exp(b)).astype(BF16)
    kd = (k * jnp.exp(bend - b)).astype(BF16)
    ebend = jnp.exp(bend3)
    return qe, kd, ebend, xs


def _level_masks(chunk):
    ri = lax.broadcasted_iota(jnp.int32, (chunk, chunk), 0)
    ci = lax.broadcasted_iota(jnp.int32, (chunk, chunk), 1)
    masks = []
    for l in range(chunk.bit_length() - 1):
        m = 1 << l
        same_parent = (ri >> (l + 1)) == (ci >> (l + 1))
        masks.append(same_parent & ((ri & m) != 0) & ((ci & m) == 0))
    return masks, ri == ci


def _recurrence_tile(q, k, v, g, st_ref, n_heads, chunk, masks, eye):
    n = q.shape[0]
    dk = q.shape[1] // n_heads
    dv = v.shape[1] // n_heads
    qe, kd, ebend, xs = _level_operands(q, k, g, chunk)
    vb = v.astype(BF16)
    qk = q * k
    outs = []
    for h in range(n_heads):
        ks = slice(h * dk, (h + 1) * dk)
        vs = slice(h * dv, (h + 1) * dv)
        st = st_ref[h]
        o_chunks = []
        for c in range(n // chunk):
            rs = slice(c * chunk, (c + 1) * chunk)
            dq = jnp.sum(qk[rs, ks], axis=-1, keepdims=True)
            scores = jnp.where(eye, dq, 0.0)
            for l, x in enumerate(xs):
                xl = x[rs, ks]
                scores = scores + jnp.where(masks[l], _dot_nt(xl, xl), 0.0)
            vc = vb[rs, vs]
            o = _dot_nt(qe[rs, ks], st.astype(BF16)) + _dot(scores.astype(BF16), vc)
            o_chunks.append(o)
            st = st * ebend[c][:, ks] + _dot_tn(vc, kd[rs, ks])
        st_ref[h] = st
        outs.append(jnp.concatenate(o_chunks, axis=0) if len(o_chunks) > 1 else o_chunks[0])
    return outs


def _linear_scan(a, u):
    n = a.shape[0]
    row = lax.broadcasted_iota(jnp.int32, a.shape, 0)
    s = 1
    while s < n:
        valid = row >= s
        a_sh = pltpu.roll(a, s, 0)
        u_sh = pltpu.roll(u, s, 0)
        u = jnp.where(valid, a * u_sh + u, u)
        a = jnp.where(valid, a * a_sh, a)
        s *= 2
    return a, u


def _mixer_kernel(x_ref, sh_ref, sc_ref, gt_ref, gain_ref, win_ref, lbp_ref, hn_ref,
                  cw_ref, rgp_ref, rwa_ref, rwx_ref, wal_ref, bal_ref, gn_ref,
                  wbr_ref, wout_ref,
                  o_ref, sa_ref, cv_ref, hr_ref, sg_ref,
                  sta, stc, cvs, hrs, *, dims):
    d_a, d_b, dk_tot, d_c = dims
    j = pl.program_id(1)
    nj = pl.num_programs(1)

    @pl.when(j == 0)
    def _():
        sta[...] = jnp.zeros_like(sta)
        stc[...] = jnp.zeros_like(stc)
        cvs[...] = jnp.zeros_like(cvs)
        hrs[...] = jnp.zeros_like(hrs)

    x = x_ref[...]
    n = x.shape[0]
    h = _ada_norm(x, gain_ref[...], sh_ref[0], sc_ref[0]).astype(BF16)
    masks, eye = _level_masks(REC_CHUNK)
    o_a = 0
    o_b = o_a + 4 * d_a
    o_c = o_b + 2 * d_b
    o_m = o_c + 2 * dk_tot + 2 * d_c
    o_ca = o_m + 3 * x.shape[1]

    za = _dot(h, win_ref[:, o_a:o_b])
    lbp = lbp_ref[...]
    q, k, log_f = _hgrn_gates(za[:, 0:d_a], za[:, d_a:2 * d_a], lbp[0:1], lbp[1:2], lbp[2:3])
    oa = _recurrence_tile(q, k, za[:, 2 * d_a:3 * d_a], log_f, sta, H_A, REC_CHUNK, masks, eye)
    y_a = _head_out(oa, hn_ref[...], za[:, 3 * d_a:4 * d_a])

    zb = _dot(h, win_ref[:, o_b:o_c])
    bx = zb[:, 0:d_b]
    cw = cw_ref[...]
    rgp = rgp_ref[...]
    ext = jnp.concatenate([cvs[...], bx], axis=0)
    conv = rgp[0:1] + cw[3:4] * bx
    for jj in range(CONV_W - 1):
        off = SUBLANES - (CONV_W - 1) + jj
        conv = conv + cw[jj:jj + 1] * ext[off:off + n]
    cvs[...] = bx[n - SUBLANES:n]
    a, u = _rglru_gates(conv, rwa_ref, rwx_ref, rgp[1:2], rgp[2:3], rgp[3:4])
    a_cum, u_cum = _linear_scan(a, u)
    hseq = u_cum + a_cum * hrs[0:1]
    hrs[...] = jnp.broadcast_to(hseq[n - 1:n], hrs.shape)
    y_b = _gelu_tanh(zb[:, d_b:2 * d_b]) * hseq

    zc = _dot(h, win_ref[:, o_c:o_m])
    c_a = _dot(h, win_ref[:, o_ca:o_ca + LANES])
    log_fc = _gla_log_f(c_a, wal_ref, bal_ref[...])
    dk_c = dk_tot // H_C
    oc = _recurrence_tile(zc[:, 0:dk_tot] * (dk_c ** -0.5), zc[:, dk_tot:2 * dk_tot],
                          zc[:, 2 * dk_tot:2 * dk_tot + d_c], log_fc, stc, H_C, REC_CHUNK, masks, eye)
    y_c = _head_out(oc, gn_ref[...], zc[:, 2 * dk_tot + d_c:2 * dk_tot + 2 * d_c])

    zm = _dot(h, win_ref[:, o_m:o_ca])
    o_ref[...] = _merge(x, gt_ref[0], zm, y_a, y_b, y_c, wbr_ref, wout_ref)

    @pl.when(j == nj - 1)
    def _():
        for hh in range(H_A):
            sa_ref[hh] = sta[hh].T
        for hp in range(H_C // 2):
            pair = jnp.concatenate([stc[2 * hp], stc[2 * hp + 1]], axis=1).T
            sg_ref[2 * hp] = pair[0:dk_c]
            sg_ref[2 * hp + 1] = pair[dk_c:2 * dk_c]
        cv_ref[...] = bx[n - (CONV_W - 1):n]
        hr_ref[...] = hseq[n - 1:n]


def _mixer_call(x, mod4, gain, win, lbp, hnorm, cw, rgp, rwa, rwx, wal, bal, gnorm, wbr, wout,
                layer, batch, seq, dims):
    n, d = x.shape
    d_a, d_b, dk_tot, d_c = dims
    tile = min(MIX_TILE, seq)
    assert seq % tile == 0 and tile % REC_CHUNK == 0
    tpb = seq // tile
    dk_a, dv_a = d_a // H_A, d_a // H_A
    dk_c, dv_c = dk_tot // H_C, d_c // H_C
    nin = win.shape[2]
    xmap = lambda b, j: (b * tpb + j, 0)
    mspec = lambda col: pl.BlockSpec((None, 1, 1, d), lambda b, j: (layer, b, 0, col))
    lsel = lambda *rest: (lambda b, j: (layer,) + rest)
    in_specs = [
        pl.BlockSpec((tile, d), xmap),
        mspec(3), mspec(4), mspec(5),
        pl.BlockSpec((None, None, 1, d), lsel(1, 0, 0)),
        pl.BlockSpec((None, d, nin), lsel(0, 0), pipeline_mode=pl.Buffered(1)),
        pl.BlockSpec((None, 3, d_a), lsel(0, 0)),
        pl.BlockSpec((None, 1, dv_a), lsel(0, 0)),
        pl.BlockSpec((None, CONV_W, d_b), lsel(0, 0)),
        pl.BlockSpec((None, 4, d_b), lsel(0, 0)),
        pl.BlockSpec((None, NB_B, d_b // NB_B, d_b // NB_B), lsel(0, 0, 0)),
        pl.BlockSpec((None, NB_B, d_b // NB_B, d_b // NB_B), lsel(0, 0, 0)),
        pl.BlockSpec((None, LANES, dk_tot), lsel(0, 0)),
        pl.BlockSpec((None, 1, dk_tot), lsel(0, 0)),
        pl.BlockSpec((None, 1, dv_c), lsel(0, 0)),
        pl.BlockSpec((None, 3, d_a, d), lsel(0, 0, 0), pipeline_mode=pl.Buffered(1)),
        pl.BlockSpec((None, d, d), lsel(0, 0), pipeline_mode=pl.Buffered(1)),
    ]
    out_specs = [
        pl.BlockSpec((tile, d), xmap),
        pl.BlockSpec((None, H_A, dk_a, dv_a), lambda b, j: (b, 0, 0, 0)),
        pl.BlockSpec((None, CONV_W - 1, d_b), lambda b, j: (b, 0, 0)),
        pl.BlockSpec((None, 1, d_b), lambda b, j: (b, 0, 0)),
        pl.BlockSpec((None, H_C, dk_c, dv_c), lambda b, j: (b, 0, 0, 0)),
    ]
    out_shape = [
        jax.ShapeDtypeStruct((n, d), F32),
        jax.ShapeDtypeStruct((batch, H_A, dk_a, dv_a), F32),
        jax.ShapeDtypeStruct((batch, CONV_W - 1, d_b), F32),
        jax.ShapeDtypeStruct((batch, 1, d_b), F32),
        jax.ShapeDtypeStruct((batch, H_C, dk_c, dv_c), F32),
    ]
    scratch = [
        pltpu.VMEM((H_A, dv_a, dk_a), F32),
        pltpu.VMEM((H_C, dv_c, dk_c), F32),
        pltpu.VMEM((SUBLANES, d_b), F32),
        pltpu.VMEM((SUBLANES, d_b), F32),
    ]
    return pl.pallas_call(
        functools.partial(_mixer_kernel, dims=dims),
        grid=(batch, tpb),
        in_specs=in_specs,
        out_specs=out_specs,
        out_shape=out_shape,
        scratch_shapes=scratch,
        compiler_params=pltpu.CompilerParams(
            dimension_semantics=("arbitrary", "arbitrary"), vmem_limit_bytes=VMEM_LIMIT_BYTES),
        name="mixer_prompt",
    )(x, mod4, mod4, mod4, gain, win, lbp, hnorm, cw, rgp, rwa, rwx, wal, bal, gnorm, wbr, wout)


def _dec_pre_kernel(x_ref, sh_ref, sc_ref, gain_ref, win_ref, lbp_ref, cw_ref, rgp_ref,
                    rwa_ref, rwx_ref, wal_ref, bal_ref, cvin_ref, hrin_ref,
                    sv_ref, post_ref, cvout_ref, hrout_ref, *, dims):
    d_a, d_b, dk_tot, d_c = dims
    x = x_ref[...]
    d = x.shape[1]
    h = _ada_norm(x, gain_ref[...], sh_ref[0], sc_ref[0]).astype(BF16)
    o_b = 4 * d_a
    o_c = o_b + 2 * d_b
    o_m = o_c + 2 * dk_tot + 2 * d_c
    o_ca = o_m + 3 * d

    za = _dot(h, win_ref[:, 0:o_b])
    lbp = lbp_ref[...]
    q, k, log_f = _hgrn_gates(za[:, 0:d_a], za[:, d_a:2 * d_a], lbp[0:1], lbp[1:2], lbp[2:3])

    zb = _dot(h, win_ref[:, o_b:o_c])
    bx = zb[:, 0:d_b]
    cw = cw_ref[...]
    rgp = rgp_ref[...]
    cvin = cvin_ref[...]
    conv = rgp[0:1] + cw[3:4] * bx
    for jj in range(CONV_W - 1):
        conv = conv + cw[jj:jj + 1] * cvin[:, jj * d_b:(jj + 1) * d_b]
    a, u = _rglru_gates(conv, rwa_ref, rwx_ref, rgp[1:2], rgp[2:3], rgp[3:4])
    hnew = u + a * hrin_ref[...]
    y_b = _gelu_tanh(zb[:, d_b:2 * d_b]) * hnew
    cvout_ref[...] = jnp.concatenate([cvin[:, d_b:(CONV_W - 1) * d_b], bx], axis=-1)
    hrout_ref[...] = hnew

    zc = _dot(h, win_ref[:, o_c:o_m])
    c_a = _dot(h, win_ref[:, o_ca:o_ca + LANES])
    log_fc = _gla_log_f(c_a, wal_ref, bal_ref[...])
    dk_c = dk_tot // H_C
    zm = _dot(h, win_ref[:, o_m:o_ca])

    sv_ref[...] = jnp.concatenate(
        [q, jnp.exp(log_f), k, za[:, 2 * d_a:3 * d_a],
         zc[:, 0:dk_tot] * (dk_c ** -0.5), jnp.exp(log_fc), zc[:, dk_tot:2 * dk_tot],
         zc[:, 2 * dk_tot:2 * dk_tot + d_c]], axis=-1)
    post_ref[...] = jnp.concatenate(
        [za[:, 3 * d_a:4 * d_a], y_b, zc[:, 2 * dk_tot + d_c:2 * dk_tot + 2 * d_c], zm], axis=-1)


def _dec_pre_call(x, mod4, gain, win, lbp, cw, rgp, rwa, rwx, wal, bal, cv_all, hr_all, layer, dims):
    n, d = x.shape
    d_a, d_b, dk_tot, d_c = dims
    nin = win.shape[2]
    sv_w = 4 * d_a + 3 * dk_tot + d_c
    post_w = d_a + d_b + d_c + 3 * d
    mspec = lambda col: pl.BlockSpec((None, 1, n, d), lambda i: (layer, 0, 0, col))
    lsel = lambda *rest: (lambda i: (layer,) + rest)
    in_specs = [
        pl.BlockSpec((n, d), lambda i: (0, 0)),
        mspec(3), mspec(4),
        pl.BlockSpec((None, None, 1, d), lsel(1, 0, 0)),
        pl.BlockSpec((None, d, nin), lsel(0, 0), pipeline_mode=pl.Buffered(1)),
        pl.BlockSpec((None, 3, d_a), lsel(0, 0)),
        pl.BlockSpec((None, CONV_W, d_b), lsel(0, 0)),
        pl.BlockSpec((None, 4, d_b), lsel(0, 0)),
        pl.BlockSpec((None, NB_B, d_b // NB_B, d_b // NB_B), lsel(0, 0, 0)),
        pl.BlockSpec((None, NB_B, d_b // NB_B, d_b // NB_B), lsel(0, 0, 0)),
        pl.BlockSpec((None, LANES, dk_tot), lsel(0, 0)),
        pl.BlockSpec((None, 1, dk_tot), lsel(0, 0)),
        pl.BlockSpec((None, n, (CONV_W - 1) * d_b), lsel(0, 0)),
        pl.BlockSpec((None, n, d_b), lsel(0, 0)),
    ]
    full = lambda w: pl.BlockSpec((n, w), lambda i: (0, 0))
    return pl.pallas_call(
        functools.partial(_dec_pre_kernel, dims=dims),
        grid=(1,),
        in_specs=in_specs,
        out_specs=[full(sv_w), full(post_w), full((CONV_W - 1) * d_b), full(d_b)],
        out_shape=[jax.ShapeDtypeStruct((n, sv_w), F32), jax.ShapeDtypeStruct((n, post_w), F32),
                   jax.ShapeDtypeStruct((n, (CONV_W - 1) * d_b), F32),
                   jax.ShapeDtypeStruct((n, d_b), F32)],
        compiler_params=pltpu.CompilerParams(
            dimension_semantics=("arbitrary",), vmem_limit_bytes=VMEM_LIMIT_BYTES),
        name="mixer_sample_pre",
    )(x, mod4, mod4, gain, win, lbp, cw, rgp, rwa, rwx, wal, bal, cv_all, hr_all)


def _columns(rows):
    pad = jnp.zeros((LANES - rows.shape[0], LANES), F32)
    return jnp.concatenate([rows, pad], axis=0).T


def _dec_state_kernel(sv_ref, sa_ref, sg_ref, oa_ref, oc_ref, sa_out, sg_out, *, dims):
    d_a, d_b, dk_tot, d_c = dims
    sv = sv_ref[...]
    g = sv.shape[0]
    dk_a = d_a // H_A
    dk_c = dk_tot // H_C
    dv_c = d_c // H_C
    o_q, o_f, o_k, o_v = 0, d_a, 2 * d_a, 3 * d_a
    for hh in range(H_A):
        qt = _columns(sv[:, o_q + hh * dk_a:o_q + (hh + 1) * dk_a])
        ft = _columns(sv[:, o_f + hh * dk_a:o_f + (hh + 1) * dk_a])
        kt = _columns(sv[:, o_k + hh * dk_a:o_k + (hh + 1) * dk_a])
        for jj in range(g):
            v = sv[jj:jj + 1, o_v + hh * dk_a:o_v + (hh + 1) * dk_a]
            s_new = ft[:, jj:jj + 1] * sa_ref[jj, hh] + kt[:, jj:jj + 1] * v
            sa_out[jj, hh] = s_new
            oa_ref[jj:jj + 1, hh * dk_a:(hh + 1) * dk_a] = jnp.sum(
                qt[:, jj:jj + 1] * s_new, axis=0, keepdims=True)
    c_q = 4 * d_a
    c_f = c_q + dk_tot
    c_k = c_f + dk_tot
    c_v = c_k + dk_tot
    per_tile = LANES // dk_c
    for tp in range(dk_tot // LANES):
        qt = _columns(sv[:, c_q + tp * LANES:c_q + (tp + 1) * LANES])
        ft = _columns(sv[:, c_f + tp * LANES:c_f + (tp + 1) * LANES])
        kt = _columns(sv[:, c_k + tp * LANES:c_k + (tp + 1) * LANES])
        for hp in range(per_tile):
            hh = tp * per_tile + hp
            rs = slice(hp * dk_c, (hp + 1) * dk_c)
            for jj in range(g):
                v = sv[jj:jj + 1, c_v + hh * dv_c:c_v + (hh + 1) * dv_c]
                s_new = ft[rs, jj:jj + 1] * sg_ref[jj, hh] + kt[rs, jj:jj + 1] * v
                sg_out[jj, hh] = s_new
                oc_ref[jj:jj + 1, hh * dv_c:(hh + 1) * dv_c] = jnp.sum(
                    qt[rs, jj:jj + 1] * s_new, axis=0, keepdims=True)


def _dec_state_call(sv, st_a, st_c, layer, dims):
    n, sv_w = sv.shape
    d_a, d_b, dk_tot, d_c = dims
    g = DEC_GROUP
    assert n % g == 0
    dk_a = d_a // H_A
    dk_c, dv_c = dk_tot // H_C, d_c // H_C
    return pl.pallas_call(
        functools.partial(_dec_state_kernel, dims=dims),
        grid=(n // g,),
        in_specs=[
            pl.BlockSpec((g, sv_w), lambda i: (i, 0)),
            pl.BlockSpec((None, g, H_A, dk_a, dk_a), lambda i: (layer, i, 0, 0, 0)),
            pl.BlockSpec((None, g, H_C, dk_c, dv_c), lambda i: (layer, i, 0, 0, 0)),
        ],
        out_specs=[
            pl.BlockSpec((g, d_a), lambda i: (i, 0)),
            pl.BlockSpec((g, d_c), lambda i: (i, 0)),
            pl.BlockSpec((g, H_A, dk_a, dk_a), lambda i: (i, 0, 0, 0)),
            pl.BlockSpec((g, H_C, dk_c, dv_c), lambda i: (i, 0, 0, 0)),
        ],
        out_shape=[
            jax.ShapeDtypeStruct((n, d_a), F32),
            jax.ShapeDtypeStruct((n, d_c), F32),
            jax.ShapeDtypeStruct((n, H_A, dk_a, dk_a), F32),
            jax.ShapeDtypeStruct((n, H_C, dk_c, dv_c), F32),
        ],
        compiler_params=pltpu.CompilerParams(
            dimension_semantics=("arbitrary",), vmem_limit_bytes=VMEM_LIMIT_BYTES),
        name="mixer_sample_state",
    )(sv, st_a, st_c)


def _dec_post_kernel(x_ref, gt_ref, post_ref, oa_ref, oc_ref, hn_ref, gn_ref, wbr_ref, wout_ref,
                     o_ref, *, dims):
    d_a, d_b, dk_tot, d_c = dims
    x = x_ref[...]
    post = post_ref[...]
    oa = oa_ref[...]
    oc = oc_ref[...]
    dv_a = d_a // H_A
    dv_c = d_c // H_C
    y_a = _head_out([oa[:, hh * dv_a:(hh + 1) * dv_a] for hh in range(H_A)], hn_ref[...],
                    post[:, 0:d_a])
    y_b = post[:, d_a:d_a + d_b]
    y_c = _head_out([oc[:, hh * dv_c:(hh + 1) * dv_c] for hh in range(H_C)], gn_ref[...],
                    post[:, d_a + d_b:d_a + d_b + d_c])
    zm = post[:, d_a + d_b + d_c:]
    o_ref[...] = _merge(x, gt_ref[0], zm, y_a, y_b, y_c, wbr_ref, wout_ref)


def _dec_post_call(x, mod4, post, oa, oc, hnorm, gnorm, wbr, wout, layer, dims):
    n, d = x.shape
    d_a, d_b, dk_tot, d_c = dims
    lsel = lambda *rest: (lambda i: (layer,) + rest)
    full = lambda w: pl.BlockSpec((n, w), lambda i: (0, 0))
    return pl.pallas_call(
        functools.partial(_dec_post_kernel, dims=dims),
        grid=(1,),
        in_specs=[
            full(d),
            pl.BlockSpec((None, 1, n, d), lambda i: (layer, 0, 0, 5)),
            full(post.shape[1]), full(d_a), full(d_c),
            pl.BlockSpec((None, 1, d_a // H_A), lsel(0, 0)),
            pl.BlockSpec((None, 1, d_c // H_C), lsel(0, 0)),
            pl.BlockSpec((None, 3, d_a, d), lsel(0, 0, 0), pipeline_mode=pl.Buffered(1)),
            pl.BlockSpec((None, d, d), lsel(0, 0), pipeline_mode=pl.Buffered(1)),
        ],
        out_specs=full(d),
        out_shape=jax.ShapeDtypeStruct((n, d), F32),
        compiler_params=pltpu.CompilerParams(
            dimension_semantics=("arbitrary",), vmem_limit_bytes=VMEM_LIMIT_BYTES),
        name="mixer_sample_post",
    )(x, mod4, post, oa, oc, hnorm, gnorm, wbr, wout)


def kernel(x_prompt, x_sample, c_prompt, c_sample, state_hgrn, state_conv, state_rglru, state_gla,
           w_ada, b_ada, norm_gain, w_ffn_in, w_ffn_out, w_in, hgrn_lb_logits, hgrn_norm,
           conv_w, conv_b, rg_wa, rg_ba, rg_wx, rg_bx, rg_lambda, gla_w_alpha, gla_b_alpha,
           gla_norm, w_br_a, w_br_b, w_br_c, w_out, final_norm):
    batch, seq, d = x_prompt.shape
    nd = x_sample.shape[0]
    depth = w_in.shape[0]
    d_a = hgrn_lb_logits.shape[1]
    d_b = conv_b.shape[1]
    dk_tot = gla_b_alpha.shape[1]
    d_c = w_br_c.shape[1]
    dims = (d_a, d_b, dk_tot, d_c)
    nm = w_ada.shape[2]

    wi = w_ffn_in.astype(BF16)
    wo = w_ffn_out.astype(BF16)
    o_ca = 4 * d_a + 2 * d_b + 2 * dk_tot + 2 * d_c
    win = jnp.concatenate(
        [w_in[:, :, :o_ca], w_in[:, :, o_ca + GATE_RANK:], w_in[:, :, o_ca:o_ca + GATE_RANK],
         jnp.zeros((depth, d, LANES - GATE_RANK), w_in.dtype)], axis=-1).astype(BF16)
    wal = jnp.concatenate(
        [gla_w_alpha, jnp.zeros((depth, LANES - GATE_RANK, dk_tot), gla_w_alpha.dtype)],
        axis=1).astype(BF16)
    bal = gla_b_alpha.reshape(depth, 1, dk_tot)
    wbr = jnp.stack([w_br_a, w_br_b, w_br_c], axis=1).astype(BF16)
    wout = w_out.astype(BF16)
    rwa = rg_wa.astype(BF16)
    rwx = rg_wx.astype(BF16)
    rgp = jnp.stack([conv_b, rg_ba, rg_bx, rg_lambda], axis=1)
    hnorm = hgrn_norm.reshape(depth, 1, -1)
    gnorm = gla_norm.reshape(depth, 1, -1)
    lb = jnp.cumsum(jax.nn.softmax(hgrn_lb_logits.astype(F32), axis=0), axis=0)
    lb = lb - lb[0]
    lbp = jnp.stack([jnp.log(lb), jnp.log1p(-lb), 1.0 - lb], axis=1)

    norm_gain = norm_gain.reshape(depth, N_SUB, 1, d)
    c_all = jnp.concatenate([c_prompt, c_sample], axis=0)
    mod = _ada_call(c_all, w_ada, b_ada)
    mod_p = mod[:, :batch].reshape(depth, batch, 1, nm)
    mod_s = mod[:, batch:].reshape(depth, 1, nd, nm)

    ffn_tile = min(FFN_TILE, seq)
    tpg = seq // ffn_tile
    x = x_prompt.reshape(batch * seq, d)
    sa_p, cv_p, hr_p, sg_p = [], [], [], []
    for l in range(depth):
        x = _ffn_call(x, mod_p, norm_gain, wi, wo, l, 0, 0, ffn_tile, tpg)
        x, sa, cv, hr, sg = _mixer_call(x, mod_p, norm_gain, win, lbp, hnorm, conv_w, rgp, rwa, rwx,
                                        wal, bal, gnorm, wbr, wout, l, batch, seq, dims)
        x = _ffn_call(x, mod_p, norm_gain, wi, wo, l, 1, 2, ffn_tile, tpg,
                      final_gain=final_norm if l == depth - 1 else None)
        sa_p.append(sa); cv_p.append(cv); hr_p.append(hr[:, 0]); sg_p.append(sg)
    y_prompt = x.reshape(batch, seq, d)

    xs = x_sample.reshape(nd, d)
    cv_all = state_conv.reshape(depth, nd, (CONV_W - 1) * d_b)
    sa_s, cv_s, hr_s, sg_s = [], [], [], []
    for l in range(depth):
        xs = _ffn_call(xs, mod_s, norm_gain, wi, wo, l, 0, 0, nd, 1)
        sv, post, cv, hr = _dec_pre_call(xs, mod_s, norm_gain, win, lbp, conv_w, rgp, rwa, rwx,
                                         wal, bal, cv_all, state_rglru, l, dims)
        oa, oc, sa, sg = _dec_state_call(sv, state_hgrn, state_gla, l, dims)
        xs = _dec_post_call(xs, mod_s, post, oa, oc, hnorm, gnorm, wbr, wout, l, dims)
        xs = _ffn_call(xs, mod_s, norm_gain, wi, wo, l, 1, 2, nd, 1,
                       final_gain=final_norm if l == depth - 1 else None)
        sa_s.append(sa); cv_s.append(cv.reshape(nd, CONV_W - 1, d_b)); hr_s.append(hr); sg_s.append(sg)
    y_sample = xs.reshape(nd, 1, d)

    st = jnp.stack
    return (y_prompt, y_sample, st(sa_p), st(sa_s), st(cv_p), st(cv_s), st(hr_p), st(hr_s),
            st(sg_p), st(sg_s))
```

```python
import functools
import math

import jax
import jax.numpy as jnp
from jax import lax
from jax.experimental import pallas as pl
from jax.experimental.pallas import tpu as pltpu

F32 = jnp.float32
BF16 = jnp.bfloat16

EPS = 1e-6
H_A = 4
NB_B = 4
CONV_W = 4
RG_C = 8.0
H_C = 4
GATE_RANK = 16
GLA_TAU = 16.0
N_SUB = 3
LOG2E = 1.4426950408889634

LANES = 128
SUBLANES = 8
VMEM_LIMIT_BYTES = 56 * 1024 * 1024

REC_CHUNK = 128
MIX_TILE = 256
FFN_TILE = 512
DEC_GROUP = 8


def _dot(a, b):
    return jnp.dot(a, b, preferred_element_type=F32)


def _dot_nt(a, b):
    return lax.dot_general(a, b, (((1,), (1,)), ((), ())), preferred_element_type=F32)


def _dot_tn(a, b):
    return lax.dot_general(a, b, (((0,), (0,)), ((), ())), preferred_element_type=F32)


def _sigmoid(x):
    return 0.5 * jnp.tanh(0.5 * x) + 0.5


def _silu(x):
    h = 0.5 * x
    return h * jnp.tanh(h) + h


def _softplus(x):
    return jnp.maximum(x, 0.0) + jnp.log(1.0 + jnp.exp(-jnp.abs(x)))


def _log_sigmoid(x):
    return -_softplus(-x)


def _gelu_tanh(x):
    c = math.sqrt(2.0 / math.pi)
    return x * (0.5 * (1.0 + jnp.tanh(c * (x + 0.044715 * (x * x * x)))))


def _rms(x, gain):
    ms = jnp.mean(x * x, axis=-1, keepdims=True)
    return x * lax.rsqrt(ms + EPS) * gain


def _ada_norm(x, gain, shift, scale):
    return _rms(x, gain) * (1.0 + scale) + shift


def _ada_kernel(c_ref, w_ref, b_ref, o_ref):
    o_ref[...] = _dot(c_ref[...].astype(BF16), w_ref[...].astype(BF16)) + b_ref[...]


def _ada_call(c_all, w_ada, b_ada):
    depth, d, nm = w_ada.shape
    rows = c_all.shape[0]
    tn = 1536
    assert nm % tn == 0
    return pl.pallas_call(
        _ada_kernel,
        grid=(depth, nm // tn),
        in_specs=[
            pl.BlockSpec((rows, d), lambda l, n: (0, 0)),
            pl.BlockSpec((None, d, tn), lambda l, n: (l, 0, n)),
            pl.BlockSpec((None, 1, tn), lambda l, n: (l, 0, n)),
        ],
        out_specs=pl.BlockSpec((None, rows, tn), lambda l, n: (l, 0, n)),
        out_shape=jax.ShapeDtypeStruct((depth, rows, nm), F32),
        compiler_params=pltpu.CompilerParams(
            dimension_semantics=("arbitrary", "arbitrary"),
            vmem_limit_bytes=VMEM_LIMIT_BYTES),
        name="ada_mod",
    )(c_all, w_ada, b_ada.reshape(depth, 1, nm))


def _ffn_kernel(x_ref, sh_ref, sc_ref, gt_ref, gain_ref, wi_ref, wo_ref, *rest, n_chunks, final):
    if final:
        fg_ref, o_ref = rest
    else:
        (o_ref,) = rest
    x = x_ref[...]
    h = _ada_norm(x, gain_ref[...], sh_ref[0], sc_ref[0]).astype(BF16)
    f = wo_ref.shape[0]
    fc = f // n_chunks
    acc = None
    for c in range(n_chunks):
        g = _dot(h, wi_ref[:, c * fc:(c + 1) * fc])
        u = _dot(h, wi_ref[:, f + c * fc:f + (c + 1) * fc])
        a = (_silu(g) * u).astype(BF16)
        part = _dot(a, wo_ref[c * fc:(c + 1) * fc, :])
        acc = part if acc is None else acc + part
    y = x + (0.5 * gt_ref[0]) * acc
    if final:
        y = _rms(y, fg_ref[...])
    o_ref[...] = y


def _mod_spec(rows_per_group, d, layer, col, group_of):
    return pl.BlockSpec((None, 1, rows_per_group, d), lambda i: (layer, group_of(i), 0, col))


def _ffn_call(x, mod4, gain, wi, wo, layer, which, sub, tile, tiles_per_group, final_gain=None):
    n, d = x.shape
    f = wo.shape[2]
    r = mod4.shape[2]
    group_of = (lambda i: i // tiles_per_group)
    final = final_gain is not None
    in_specs = [
        pl.BlockSpec((tile, d), lambda i: (i, 0)),
        _mod_spec(r, d, layer, 3 * sub + 0, group_of),
        _mod_spec(r, d, layer, 3 * sub + 1, group_of),
        _mod_spec(r, d, layer, 3 * sub + 2, group_of),
        pl.BlockSpec((None, None, 1, d), lambda i: (layer, sub, 0, 0)),
        pl.BlockSpec((None, None, d, 2 * f), lambda i: (layer, which, 0, 0),
                     pipeline_mode=pl.Buffered(1)),
        pl.BlockSpec((None, None, f, d), lambda i: (layer, which, 0, 0),
                     pipeline_mode=pl.Buffered(1)),
    ]
    args = [x, mod4, mod4, mod4, gain, wi, wo]
    if final:
        in_specs.append(pl.BlockSpec((1, d), lambda i: (0, 0)))
        args.append(final_gain.reshape(1, d))
    return pl.pallas_call(
        functools.partial(_ffn_kernel, n_chunks=1, final=final),
        grid=(n // tile,),
        in_specs=in_specs,
        out_specs=pl.BlockSpec((tile, d), lambda i: (i, 0)),
        out_shape=jax.ShapeDtypeStruct((n, d), F32),
        compiler_params=pltpu.CompilerParams(
            dimension_semantics=("arbitrary",), vmem_limit_bytes=VMEM_LIMIT_BYTES),
        name="ffn",
    )(*args)


def _hgrn_gates(a_q, a_f, loglb, log1mlb, omlb):
    q = _silu(a_q)
    c = log1mlb + _log_sigmoid(a_f)
    m = jnp.maximum(loglb, c)
    log_f = m + jnp.log(1.0 + jnp.exp(-jnp.abs(loglb - c)))
    k = omlb * _sigmoid(-a_f)
    return q, k, log_f


def _gla_log_f(c_a, wal_ref, bal):
    return _log_sigmoid(_dot(c_a.astype(BF16), wal_ref[...]) + bal) * (1.0 / GLA_TAU)


def _rglru_gates(conv, wa_ref, wx_ref, ba, bx, lam):
    cb = conv.astype(BF16)
    bs = conv.shape[1] // NB_B
    r = jnp.concatenate([_dot(cb[:, n * bs:(n + 1) * bs], wa_ref[n]) for n in range(NB_B)], axis=-1)
    i = jnp.concatenate([_dot(cb[:, n * bs:(n + 1) * bs], wx_ref[n]) for n in range(NB_B)], axis=-1)
    r = _sigmoid(r + ba)
    i = _sigmoid(i + bx)
    log_a = (-RG_C) * r * _softplus(-lam)
    a = jnp.exp(log_a)
    u = jnp.sqrt(1.0 - jnp.exp(2.0 * log_a)) * i * conv
    return a, u


def _head_out(o_heads, gain, gate):
    return jnp.concatenate([_rms(o, gain) for o in o_heads], axis=-1) * _silu(gate)


def _merge(x, gate, zm, y_a, y_b, y_c, wbr_ref, wout_ref):
    d = x.shape[1]
    merged = (_sigmoid(zm[:, 0:d]) * _dot(y_a.astype(BF16), wbr_ref[0])
              + _sigmoid(zm[:, d:2 * d]) * _dot(y_b.astype(BF16), wbr_ref[1])
              + _sigmoid(zm[:, 2 * d:3 * d]) * _dot(y_c.astype(BF16), wbr_ref[2]))
    return x + gate * _dot(merged.astype(BF16), wout_ref[...])


def _no_pump(count=1):
    del count


def _split3(x):
    hi = x.astype(BF16)
    r1 = x - hi.astype(F32)
    mid = r1.astype(BF16)
    lo = (r1 - mid.astype(F32)).astype(BF16)
    return hi, mid, lo


def _chunk_cumsum(g, chunk, tri):
    n = g.shape[0]
    hi, mid, lo = _split3(g)
    out = []
    for c in range(n // chunk):
        rs = slice(c * chunk, (c + 1) * chunk)
        out.append(_dot(tri, lo[rs]) + _dot(tri, mid[rs]) + _dot(tri, hi[rs]))
    return jnp.concatenate(out, axis=0) if len(out) > 1 else out[0]


def _level_operands(q, k, g, chunk, tri, pump=_no_pump):
    n, w = g.shape
    g = g * LOG2E
    rin = lax.broadcasted_iota(jnp.int32, (n, w), 0) & (chunk - 1)
    b = _chunk_cumsum(g, chunk, tri)
    xs = []
    n_levels = chunk.bit_length() - 1
    for l in range(n_levels):
        m = 1 << l
        right = (rin & m) != 0
        if l == 0:
            dlt = jnp.where(right, g, 0.0)
        elif l == 1:
            gm1 = pltpu.roll(g, 1, 0)
            gp1 = pltpu.roll(g, n - 1, 0)
            r4 = rin & 3
            dlt = jnp.where(r4 == 0, gp1, jnp.where(r4 == 1, 0.0, jnp.where(r4 == 2, g, g + gm1)))
        else:
            p = 2 * m
            b3 = b.reshape(n // p, p, w)
            ref = jnp.broadcast_to(b3[:, m - 1:m, :], (n // p, p, w)).reshape(n, w)
            dlt = -jnp.abs(b - ref)
        xs.append((jnp.where(right, q, k) * jnp.exp2(dlt)).astype(BF16))
        pump()
    b3 = b.reshape(n // chunk, chunk, w)
    bend3 = b3[:, chunk - 1:chunk, :]
    bend = jnp.broadcast_to(bend3, (n // chunk, chunk, w)).reshape(n, w)
    qe = (q * jnp.exp2(b)).astype(BF16)
    kd = (k * jnp.exp2(bend - b)).astype(BF16)
    ebend = jnp.exp2(bend3)
    return qe, kd, ebend, xs


def _level_masks(chunk):
    ri = lax.broadcasted_iota(jnp.int32, (chunk, chunk), 0)
    ci = lax.broadcasted_iota(jnp.int32, (chunk, chunk), 1)
    masks = []
    for l in range(chunk.bit_length() - 1):
        m = 1 << l
        same_parent = (ri >> (l + 1)) == (ci >> (l + 1))
        masks.append(same_parent & ((ri & m) != 0) & ((ci & m) == 0))
    tri = jnp.where(ri >= ci, 1.0, 0.0).astype(BF16)
    return masks, ri == ci, tri


def _recurrence_tile(q, k, v, g, st_ref, keep, n_heads, chunk, masks, eye, tri, pump=_no_pump):
    n = q.shape[0]
    dk = q.shape[1] // n_heads
    dv = v.shape[1] // n_heads
    qe, kd, ebend, xs = _level_operands(q, k, g, chunk, tri, pump)
    vb = v.astype(BF16)
    qk = q * k
    outs = []
    for h in range(n_heads):
        ks = slice(h * dk, (h + 1) * dk)
        vs = slice(h * dv, (h + 1) * dv)
        st = st_ref[h] * keep
        o_chunks = []
        for c in range(n // chunk):
            rs = slice(c * chunk, (c + 1) * chunk)
            dq = jnp.sum(qk[rs, ks], axis=-1, keepdims=True)
            scores = jnp.where(eye, dq, 0.0)
            for l, x in enumerate(xs):
                xl = x[rs, ks]
                scores = jnp.where(masks[l], _dot_nt(xl, xl), scores)
            vc = vb[rs, vs]
            o = _dot_nt(qe[rs, ks], st.astype(BF16)) + _dot(scores.astype(BF16), vc)
            o_chunks.append(o)
            st = st * ebend[c][:, ks] + _dot_tn(vc, kd[rs, ks])
        st_ref[h] = st
        outs.append(jnp.concatenate(o_chunks, axis=0) if len(o_chunks) > 1 else o_chunks[0])
    return outs


def _compose_scan(a, u, idx, axis, length, pump):
    s = 1
    while s < length:
        valid = idx >= s
        a_sh = pltpu.roll(a, s, axis)
        u_sh = pltpu.roll(u, s, axis)
        u = jnp.where(valid, a * u_sh + u, u)
        a = jnp.where(valid, a * a_sh, a)
        pump()
        s *= 2
    return a, u


def _linear_scan(a, u, h0, pump=_no_pump):
    n = a.shape[0]
    a_cum, u_cum = _compose_scan(a, u, lax.broadcasted_iota(jnp.int32, a.shape, 0), 0, n, pump)
    h = u_cum + a_cum * h0
    return h, h[n - 1:n]


def _mixer_kernel(xc_ref, xp_ref, sh_ref, sc_ref, gt_ref, gain_ref, win_ref, lbp_ref, hn_ref,
                  cw_ref, rgp_ref, rwa_ref, rwx_ref, wal_ref, bal_ref, gn_ref,
                  wbr_ref, wout_ref,
                  o_ref, sa_ref, cv_ref, hr_ref, sg_ref,
                  z0, z1, sta, stc, cvs, hrs, *, dims, tpb):
    d_a, d_b, dk_tot, d_c = dims
    s = pl.program_id(0)
    n, d = xc_ref.shape
    o_b = 4 * d_a
    o_c = o_b + 2 * d_b
    o_m = o_c + 2 * dk_tot + 2 * d_c
    o_ca = o_m + 3 * d
    dk_c = dk_tot // H_C
    prev = s - 1

    @pl.when(s == 0)
    def _():
        z1[...] = jnp.zeros_like(z1)
        sta[...] = jnp.zeros_like(sta)
        stc[...] = jnp.zeros_like(stc)
        cvs[...] = jnp.zeros_like(cvs)
        hrs[...] = jnp.zeros_like(hrs)

    def step(zw, zr):
        h = _ada_norm(xc_ref[...], gain_ref[...], sh_ref[0], sc_ref[0]).astype(BF16)
        piece = 2 * LANES
        todo = list(range(0, o_ca, piece)) + [o_ca]

        def pump(count=1):
            for _ in range(count):
                if todo:
                    c0 = todo.pop(0)
                    c1 = c0 + (piece if c0 < o_ca else LANES)
                    zw[:, c0:c1] = _dot(h, win_ref[:, c0:c1])

        keep = jnp.where(lax.rem(prev, tpb) == 0, 0.0, 1.0).astype(F32)
        masks, eye, tri = _level_masks(REC_CHUNK)
        x = xp_ref[...]

        lbp = lbp_ref[...]
        pump(2)
        q, k, log_f = _hgrn_gates(zr[:, 0:d_a], zr[:, d_a:2 * d_a], lbp[0:1], lbp[1:2], lbp[2:3])
        pump(2)
        oa = _recurrence_tile(q, k, zr[:, 2 * d_a:3 * d_a], log_f, sta, keep,
                              H_A, REC_CHUNK, masks, eye, tri, pump)
        y_a = _head_out(oa, hn_ref[...], zr[:, 3 * d_a:4 * d_a])
        pump(2)

        bx = zr[:, o_b:o_b + d_b]
        cw = cw_ref[...]
        rgp = rgp_ref[...]
        ext = jnp.concatenate([cvs[...] * keep, bx], axis=0)
        conv = rgp[0:1] + cw[3:4] * bx
        for jj in range(CONV_W - 1):
            off = SUBLANES - (CONV_W - 1) + jj
            conv = conv + cw[jj:jj + 1] * ext[off:off + n]
        cvs[...] = bx[n - SUBLANES:n]
        pump(2)
        a, u = _rglru_gates(conv, rwa_ref, rwx_ref, rgp[1:2], rgp[2:3], rgp[3:4])
        pump(2)
        hseq, hlast = _linear_scan(a, u, hrs[0:1] * keep, pump)
        hrs[...] = jnp.broadcast_to(hlast, hrs.shape)
        y_b = _gelu_tanh(zr[:, o_b + d_b:o_c]) * hseq

        log_fc = _gla_log_f(zr[:, o_ca:o_ca + LANES], wal_ref, bal_ref[...])
        oc = _recurrence_tile(zr[:, o_c:o_c + dk_tot] * (dk_c ** -0.5),
                              zr[:, o_c + dk_tot:o_c + 2 * dk_tot],
                              zr[:, o_c + 2 * dk_tot:o_c + 2 * dk_tot + d_c], log_fc, stc, keep,
                              H_C, REC_CHUNK, masks, eye, tri, pump)
        y_c = _head_out(oc, gn_ref[...], zr[:, o_c + 2 * dk_tot + d_c:o_m])
        pump(len(todo))

        o_ref[...] = _merge(x, gt_ref[0], zr[:, o_m:o_ca], y_a, y_b, y_c, wbr_ref, wout_ref)

    @pl.when(lax.rem(s, 2) == 0)
    def _():
        step(z0, z1)

    @pl.when(lax.rem(s, 2) == 1)
    def _():
        step(z1, z0)

    @pl.when(jnp.logical_and(s >= 1, lax.rem(prev, tpb) == tpb - 1))
    def _():
        for hh in range(H_A):
            sa_ref[hh] = sta[hh].T
        for hp in range(H_C // 2):
            pair = jnp.concatenate([stc[2 * hp], stc[2 * hp + 1]], axis=1).T
            sg_ref[2 * hp] = pair[0:dk_c]
            sg_ref[2 * hp + 1] = pair[dk_c:2 * dk_c]
        cv_ref[...] = cvs[SUBLANES - (CONV_W - 1):SUBLANES]
        hr_ref[...] = hrs[0:1]


def _mixer_call(x, mod4, gain, win, lbp, hnorm, cw, rgp, rwa, rwx, wal, bal, gnorm, wbr, wout,
                layer, batch, seq, dims):
    n, d = x.shape
    d_a, d_b, dk_tot, d_c = dims
    tile = min(MIX_TILE, seq)
    assert seq % tile == 0 and tile % REC_CHUNK == 0
    tpb = seq // tile
    dk_a, dv_a = d_a // H_A, d_a // H_A
    dk_c, dv_c = dk_tot // H_C, d_c // H_C
    nin = win.shape[2]
    nt = batch * tpb
    cur = lambda s: jnp.minimum(s, nt - 1)
    prv = lambda s: jnp.maximum(s - 1, 0)
    mspec = lambda col, t: pl.BlockSpec((None, 1, 1, d), lambda s: (layer, t(s) // tpb, 0, col))
    lsel = lambda *rest: (lambda s: (layer,) + rest)
    in_specs = [
        pl.BlockSpec((tile, d), lambda s: (cur(s), 0)),
        pl.BlockSpec((tile, d), lambda s: (prv(s), 0)),
        mspec(3, cur), mspec(4, cur), mspec(5, prv),
        pl.BlockSpec((None, None, 1, d), lsel(1, 0, 0)),
        pl.BlockSpec((None, d, nin), lsel(0, 0), pipeline_mode=pl.Buffered(1)),
        pl.BlockSpec((None, 3, d_a), lsel(0, 0)),
        pl.BlockSpec((None, 1, dv_a), lsel(0, 0)),
        pl.BlockSpec((None, CONV_W, d_b), lsel(0, 0)),
        pl.BlockSpec((None, 4, d_b), lsel(0, 0)),
        pl.BlockSpec((None, NB_B, d_b // NB_B, d_b // NB_B), lsel(0, 0, 0)),
        pl.BlockSpec((None, NB_B, d_b // NB_B, d_b // NB_B), lsel(0, 0, 0)),
        pl.BlockSpec((None, LANES, dk_tot), lsel(0, 0)),
        pl.BlockSpec((None, 1, dk_tot), lsel(0, 0)),
        pl.BlockSpec((None, 1, dv_c), lsel(0, 0)),
        pl.BlockSpec((None, 3, d_a, d), lsel(0, 0, 0), pipeline_mode=pl.Buffered(1)),
        pl.BlockSpec((None, d, d), lsel(0, 0), pipeline_mode=pl.Buffered(1)),
    ]
    out_specs = [
        pl.BlockSpec((tile, d), lambda s: (prv(s), 0)),
        pl.BlockSpec((None, H_A, dk_a, dv_a), lambda s: (prv(s) // tpb, 0, 0, 0)),
        pl.BlockSpec((None, CONV_W - 1, d_b), lambda s: (prv(s) // tpb, 0, 0)),
        pl.BlockSpec((None, 1, d_b), lambda s: (prv(s) // tpb, 0, 0)),
        pl.BlockSpec((None, H_C, dk_c, dv_c), lambda s: (prv(s) // tpb, 0, 0, 0)),
    ]
    out_shape = [
        jax.ShapeDtypeStruct((n, d), F32),
        jax.ShapeDtypeStruct((batch, H_A, dk_a, dv_a), F32),
        jax.ShapeDtypeStruct((batch, CONV_W - 1, d_b), F32),
        jax.ShapeDtypeStruct((batch, 1, d_b), F32),
        jax.ShapeDtypeStruct((batch, H_C, dk_c, dv_c), F32),
    ]
    scratch = [
        pltpu.VMEM((tile, nin), F32),
        pltpu.VMEM((tile, nin), F32),
        pltpu.VMEM((H_A, dv_a, dk_a), F32),
        pltpu.VMEM((H_C, dv_c, dk_c), F32),
        pltpu.VMEM((SUBLANES, d_b), F32),
        pltpu.VMEM((SUBLANES, d_b), F32),
    ]
    return pl.pallas_call(
        functools.partial(_mixer_kernel, dims=dims, tpb=tpb),
        grid=(nt + 1,),
        in_specs=in_specs,
        out_specs=out_specs,
        out_shape=out_shape,
        scratch_shapes=scratch,
        compiler_params=pltpu.CompilerParams(
            dimension_semantics=("arbitrary",), vmem_limit_bytes=VMEM_LIMIT_BYTES),
        name="mixer_prompt",
    )(x, x, mod4, mod4, mod4, gain, win, lbp, hnorm, cw, rgp, rwa, rwx, wal, bal, gnorm, wbr, wout)


def _dec_pre_kernel(x_ref, sh_ref, sc_ref, gain_ref, win_ref, lbp_ref, cw_ref, rgp_ref,
                    rwa_ref, rwx_ref, wal_ref, bal_ref, cvin_ref, hrin_ref,
                    sv_ref, post_ref, cvout_ref, hrout_ref, *, dims):
    d_a, d_b, dk_tot, d_c = dims
    x = x_ref[...]
    d = x.shape[1]
    h = _ada_norm(x, gain_ref[...], sh_ref[0], sc_ref[0]).astype(BF16)
    o_b = 4 * d_a
    o_c = o_b + 2 * d_b
    o_m = o_c + 2 * dk_tot + 2 * d_c
    o_ca = o_m + 3 * d

    za = _dot(h, win_ref[:, 0:o_b])
    lbp = lbp_ref[...]
    q, k, log_f = _hgrn_gates(za[:, 0:d_a], za[:, d_a:2 * d_a], lbp[0:1], lbp[1:2], lbp[2:3])

    zb = _dot(h, win_ref[:, o_b:o_c])
    bx = zb[:, 0:d_b]
    cw = cw_ref[...]
    rgp = rgp_ref[...]
    cvin = cvin_ref[...]
    conv = rgp[0:1] + cw[3:4] * bx
    for jj in range(CONV_W - 1):
        conv = conv + cw[jj:jj + 1] * cvin[:, jj * d_b:(jj + 1) * d_b]
    a, u = _rglru_gates(conv, rwa_ref, rwx_ref, rgp[1:2], rgp[2:3], rgp[3:4])
    hnew = u + a * hrin_ref[...]
    y_b = _gelu_tanh(zb[:, d_b:2 * d_b]) * hnew
    cvout_ref[...] = jnp.concatenate([cvin[:, d_b:(CONV_W - 1) * d_b], bx], axis=-1)
    hrout_ref[...] = hnew

    zc = _dot(h, win_ref[:, o_c:o_m])
    c_a = _dot(h, win_ref[:, o_ca:o_ca + LANES])
    log_fc = _gla_log_f(c_a, wal_ref, bal_ref[...])
    dk_c = dk_tot // H_C
    zm = _dot(h, win_ref[:, o_m:o_ca])

    sv_ref[...] = jnp.concatenate(
        [q, jnp.exp(log_f), k, za[:, 2 * d_a:3 * d_a],
         zc[:, 0:dk_tot] * (dk_c ** -0.5), jnp.exp(log_fc), zc[:, dk_tot:2 * dk_tot],
         zc[:, 2 * dk_tot:2 * dk_tot + d_c]], axis=-1)
    post_ref[...] = jnp.concatenate(
        [za[:, 3 * d_a:4 * d_a], y_b, zc[:, 2 * dk_tot + d_c:2 * dk_tot + 2 * d_c], zm], axis=-1)


def _dec_pre_call(x, mod4, gain, win, lbp, cw, rgp, rwa, rwx, wal, bal, cv_all, hr_all, layer, dims):
    n, d = x.shape
    d_a, d_b, dk_tot, d_c = dims
    nin = win.shape[2]
    sv_w = 4 * d_a + 3 * dk_tot + d_c
    post_w = d_a + d_b + d_c + 3 * d
    mspec = lambda col: pl.BlockSpec((None, 1, n, d), lambda i: (layer, 0, 0, col))
    lsel = lambda *rest: (lambda i: (layer,) + rest)
    in_specs = [
        pl.BlockSpec((n, d), lambda i: (0, 0)),
        mspec(3), mspec(4),
        pl.BlockSpec((None, None, 1, d), lsel(1, 0, 0)),
        pl.BlockSpec((None, d, nin), lsel(0, 0), pipeline_mode=pl.Buffered(1)),
        pl.BlockSpec((None, 3, d_a), lsel(0, 0)),
        pl.BlockSpec((None, CONV_W, d_b), lsel(0, 0)),
        pl.BlockSpec((None, 4, d_b), lsel(0, 0)),
        pl.BlockSpec((None, NB_B, d_b // NB_B, d_b // NB_B), lsel(0, 0, 0)),
        pl.BlockSpec((None, NB_B, d_b // NB_B, d_b // NB_B), lsel(0, 0, 0)),
        pl.BlockSpec((None, LANES, dk_tot), lsel(0, 0)),
        pl.BlockSpec((None, 1, dk_tot), lsel(0, 0)),
        pl.BlockSpec((None, n, (CONV_W - 1) * d_b), lsel(0, 0)),
        pl.BlockSpec((None, n, d_b), lsel(0, 0)),
    ]
    full = lambda w: pl.BlockSpec((n, w), lambda i: (0, 0))
    return pl.pallas_call(
        functools.partial(_dec_pre_kernel, dims=dims),
        grid=(1,),
        in_specs=in_specs,
        out_specs=[full(sv_w), full(post_w), full((CONV_W - 1) * d_b), full(d_b)],
        out_shape=[jax.ShapeDtypeStruct((n, sv_w), F32), jax.ShapeDtypeStruct((n, post_w), F32),
                   jax.ShapeDtypeStruct((n, (CONV_W - 1) * d_b), F32),
                   jax.ShapeDtypeStruct((n, d_b), F32)],
        compiler_params=pltpu.CompilerParams(
            dimension_semantics=("arbitrary",), vmem_limit_bytes=VMEM_LIMIT_BYTES),
        name="mixer_sample_pre",
    )(x, mod4, mod4, gain, win, lbp, cw, rgp, rwa, rwx, wal, bal, cv_all, hr_all)


def _columns(rows):
    pad = jnp.zeros((LANES - rows.shape[0], LANES), F32)
    return jnp.concatenate([rows, pad], axis=0).T


def _dec_state_kernel(sv_ref, sa_ref, sg_ref, *rest, dims):
    oa_ref, oc_ref, sa_out, sg_out = rest[-4:]
    d_a, d_b, dk_tot, d_c = dims
    sv = sv_ref[...]
    g = sv.shape[0]
    dk_a = d_a // H_A
    dk_c = dk_tot // H_C
    dv_c = d_c // H_C
    o_q, o_f, o_k, o_v = 0, d_a, 2 * d_a, 3 * d_a
    for hh in range(H_A):
        qt = _columns(sv[:, o_q + hh * dk_a:o_q + (hh + 1) * dk_a])
        ft = _columns(sv[:, o_f + hh * dk_a:o_f + (hh + 1) * dk_a])
        kt = _columns(sv[:, o_k + hh * dk_a:o_k + (hh + 1) * dk_a])
        for jj in range(g):
            v = sv[jj:jj + 1, o_v + hh * dk_a:o_v + (hh + 1) * dk_a]
            s_new = ft[:, jj:jj + 1] * sa_ref[jj, hh] + kt[:, jj:jj + 1] * v
            sa_out[jj, hh] = s_new
            oa_ref[jj:jj + 1, hh * dk_a:(hh + 1) * dk_a] = jnp.sum(
                qt[:, jj:jj + 1] * s_new, axis=0, keepdims=True)
    c_q = 4 * d_a
    c_f = c_q + dk_tot
    c_k = c_f + dk_tot
    c_v = c_k + dk_tot
    per_tile = LANES // dk_c
    for tp in range(dk_tot // LANES):
        qt = _columns(sv[:, c_q + tp * LANES:c_q + (tp + 1) * LANES])
        ft = _columns(sv[:, c_f + tp * LANES:c_f + (tp + 1) * LANES])
        kt = _columns(sv[:, c_k + tp * LANES:c_k + (tp + 1) * LANES])
        for hp in range(per_tile):
            hh = tp * per_tile + hp
            rs = slice(hp * dk_c, (hp + 1) * dk_c)
            for jj in range(g):
                v = sv[jj:jj + 1, c_v + hh * dv_c:c_v + (hh + 1) * dv_c]
                s_new = ft[rs, jj:jj + 1] * sg_ref[jj, hh] + kt[rs, jj:jj + 1] * v
                sg_out[jj, hh] = s_new
                oc_ref[jj:jj + 1, hh * dv_c:(hh + 1) * dv_c] = jnp.sum(
                    qt[rs, jj:jj + 1] * s_new, axis=0, keepdims=True)


def _dec_state_call(sv, st_a, st_c, layer, dims, stacked=None):
    n, sv_w = sv.shape
    d_a, d_b, dk_tot, d_c = dims
    depth = st_a.shape[0]
    g = DEC_GROUP
    assert n % g == 0
    dk_a = d_a // H_A
    dk_c, dv_c = dk_tot // H_C, d_c // H_C
    in_specs = [
        pl.BlockSpec((g, sv_w), lambda i: (i, 0)),
        pl.BlockSpec((None, g, H_A, dk_a, dk_a), lambda i: (layer, i, 0, 0, 0)),
        pl.BlockSpec((None, g, H_C, dk_c, dv_c), lambda i: (layer, i, 0, 0, 0)),
    ]
    args = [sv, st_a, st_c]
    aliases = {}
    if stacked is not None:
        in_specs += [pl.BlockSpec(memory_space=pl.ANY), pl.BlockSpec(memory_space=pl.ANY)]
        args += list(stacked)
        aliases = {3: 2, 4: 3}
    return pl.pallas_call(
        functools.partial(_dec_state_kernel, dims=dims),
        grid=(n // g,),
        in_specs=in_specs,
        out_specs=[
            pl.BlockSpec((g, d_a), lambda i: (i, 0)),
            pl.BlockSpec((g, d_c), lambda i: (i, 0)),
            pl.BlockSpec((None, g, H_A, dk_a, dk_a), lambda i: (layer, i, 0, 0, 0)),
            pl.BlockSpec((None, g, H_C, dk_c, dv_c), lambda i: (layer, i, 0, 0, 0)),
        ],
        out_shape=[
            jax.ShapeDtypeStruct((n, d_a), F32),
            jax.ShapeDtypeStruct((n, d_c), F32),
            jax.ShapeDtypeStruct((depth, n, H_A, dk_a, dk_a), F32),
            jax.ShapeDtypeStruct((depth, n, H_C, dk_c, dv_c), F32),
        ],
        input_output_aliases=aliases,
        compiler_params=pltpu.CompilerParams(
            dimension_semantics=("arbitrary",), vmem_limit_bytes=VMEM_LIMIT_BYTES),
        name="mixer_sample_state",
    )(*args)


def _dec_post_kernel(x_ref, gt_ref, post_ref, oa_ref, oc_ref, hn_ref, gn_ref, wbr_ref, wout_ref,
                     o_ref, *, dims):
    d_a, d_b, dk_tot, d_c = dims
    x = x_ref[...]
    post = post_ref[...]
    oa = oa_ref[...]
    oc = oc_ref[...]
    dv_a = d_a // H_A
    dv_c = d_c // H_C
    y_a = _head_out([oa[:, hh * dv_a:(hh + 1) * dv_a] for hh in range(H_A)], hn_ref[...],
                    post[:, 0:d_a])
    y_b = post[:, d_a:d_a + d_b]
    y_c = _head_out([oc[:, hh * dv_c:(hh + 1) * dv_c] for hh in range(H_C)], gn_ref[...],
                    post[:, d_a + d_b:d_a + d_b + d_c])
    zm = post[:, d_a + d_b + d_c:]
    o_ref[...] = _merge(x, gt_ref[0], zm, y_a, y_b, y_c, wbr_ref, wout_ref)


def _dec_post_call(x, mod4, post, oa, oc, hnorm, gnorm, wbr, wout, layer, dims):
    n, d = x.shape
    d_a, d_b, dk_tot, d_c = dims
    lsel = lambda *rest: (lambda i: (layer,) + rest)
    full = lambda w: pl.BlockSpec((n, w), lambda i: (0, 0))
    return pl.pallas_call(
        functools.partial(_dec_post_kernel, dims=dims),
        grid=(1,),
        in_specs=[
            full(d),
            pl.BlockSpec((None, 1, n, d), lambda i: (layer, 0, 0, 5)),
            full(post.shape[1]), full(d_a), full(d_c),
            pl.BlockSpec((None, 1, d_a // H_A), lsel(0, 0)),
            pl.BlockSpec((None, 1, d_c // H_C), lsel(0, 0)),
            pl.BlockSpec((None, 3, d_a, d), lsel(0, 0, 0), pipeline_mode=pl.Buffered(1)),
            pl.BlockSpec((None, d, d), lsel(0, 0), pipeline_mode=pl.Buffered(1)),
        ],
        out_specs=full(d),
        out_shape=jax.ShapeDtypeStruct((n, d), F32),
        compiler_params=pltpu.CompilerParams(
            dimension_semantics=("arbitrary",), vmem_limit_bytes=VMEM_LIMIT_BYTES),
        name="mixer_sample_post",
    )(x, mod4, post, oa, oc, hnorm, gnorm, wbr, wout)


def kernel(x_prompt, x_sample, c_prompt, c_sample, state_hgrn, state_conv, state_rglru, state_gla,
           w_ada, b_ada, norm_gain, w_ffn_in, w_ffn_out, w_in, hgrn_lb_logits, hgrn_norm,
           conv_w, conv_b, rg_wa, rg_ba, rg_wx, rg_bx, rg_lambda, gla_w_alpha, gla_b_alpha,
           gla_norm, w_br_a, w_br_b, w_br_c, w_out, final_norm):
    batch, seq, d = x_prompt.shape
    nd = x_sample.shape[0]
    depth = w_in.shape[0]
    d_a = hgrn_lb_logits.shape[1]
    d_b = conv_b.shape[1]
    dk_tot = gla_b_alpha.shape[1]
    d_c = w_br_c.shape[1]
    dims = (d_a, d_b, dk_tot, d_c)
    nm = w_ada.shape[2]

    wi = w_ffn_in.astype(BF16)
    wo = w_ffn_out.astype(BF16)
    o_ca = 4 * d_a + 2 * d_b + 2 * dk_tot + 2 * d_c
    w_in16 = w_in.astype(BF16)
    win = jnp.concatenate(
        [w_in16[:, :, :o_ca], w_in16[:, :, o_ca + GATE_RANK:], w_in16[:, :, o_ca:o_ca + GATE_RANK],
         jnp.zeros((depth, d, LANES - GATE_RANK), BF16)], axis=-1)
    wal = jnp.concatenate(
        [gla_w_alpha, jnp.zeros((depth, LANES - GATE_RANK, dk_tot), gla_w_alpha.dtype)],
        axis=1).astype(BF16)
    bal = gla_b_alpha.reshape(depth, 1, dk_tot)
    wbr = jnp.stack([w_br_a, w_br_b, w_br_c], axis=1).astype(BF16)
    wout = w_out.astype(BF16)
    rwa = rg_wa.astype(BF16)
    rwx = rg_wx.astype(BF16)
    rgp = jnp.stack([conv_b, rg_ba, rg_bx, rg_lambda], axis=1)
    hnorm = hgrn_norm.reshape(depth, 1, -1)
    gnorm = gla_norm.reshape(depth, 1, -1)
    lb = jnp.cumsum(jax.nn.softmax(hgrn_lb_logits.astype(F32), axis=0), axis=0)
    lb = lb - lb[0]
    lbp = jnp.stack([jnp.log(lb), jnp.log1p(-lb), 1.0 - lb], axis=1)

    norm_gain = norm_gain.reshape(depth, N_SUB, 1, d)
    c_all = jnp.concatenate([c_prompt, c_sample], axis=0)
    mod = _ada_call(c_all, w_ada, b_ada)
    mod_p = mod[:, :batch].reshape(depth, batch, 1, nm)
    mod_s = mod[:, batch:].reshape(depth, 1, nd, nm)

    ffn_tile = min(FFN_TILE, seq)
    tpg = seq // ffn_tile
    x = x_prompt.reshape(batch * seq, d)
    sa_p, cv_p, hr_p, sg_p = [], [], [], []
    for l in range(depth):
        x = _ffn_call(x, mod_p, norm_gain, wi, wo, l, 0, 0, ffn_tile, tpg)
        x, sa, cv, hr, sg = _mixer_call(x, mod_p, norm_gain, win, lbp, hnorm, conv_w, rgp, rwa, rwx,
                                        wal, bal, gnorm, wbr, wout, l, batch, seq, dims)
        x = _ffn_call(x, mod_p, norm_gain, wi, wo, l, 1, 2, ffn_tile, tpg,
                      final_gain=final_norm if l == depth - 1 else None)
        sa_p.append(sa); cv_p.append(cv); hr_p.append(hr[:, 0]); sg_p.append(sg)
    y_prompt = x.reshape(batch, seq, d)

    xs = x_sample.reshape(nd, d)
    cv_all = state_conv.reshape(depth, nd, (CONV_W - 1) * d_b)
    cv_s, hr_s = [], []
    stacked = None
    for l in range(depth):
        xs = _ffn_call(xs, mod_s, norm_gain, wi, wo, l, 0, 0, nd, 1)
        sv, post, cv, hr = _dec_pre_call(xs, mod_s, norm_gain, win, lbp, conv_w, rgp, rwa, rwx,
                                         wal, bal, cv_all, state_rglru, l, dims)
        oa, oc, sa_s, sg_s = _dec_state_call(sv, state_hgrn, state_gla, l, dims, stacked)
        stacked = (sa_s, sg_s)
        xs = _dec_post_call(xs, mod_s, post, oa, oc, hnorm, gnorm, wbr, wout, l, dims)
        xs = _ffn_call(xs, mod_s, norm_gain, wi, wo, l, 1, 2, nd, 1,
                       final_gain=final_norm if l == depth - 1 else None)
        cv_s.append(cv.reshape(nd, CONV_W - 1, d_b)); hr_s.append(hr)
    y_sample = xs.reshape(nd, 1, d)

    st = jnp.stack
    return (y_prompt, y_sample, st(sa_p), sa_s, st(cv_p), st(cv_s), st(hr_p), st(hr_s),
            st(sg_p), sg_s)
```

```python
import functools
import math

import jax
import jax.numpy as jnp
from jax import lax
from jax.experimental import pallas as pl
from jax.experimental.pallas import tpu as pltpu

F32 = jnp.float32
BF16 = jnp.bfloat16

EPS = 1e-6
H_A = 4
NB_B = 4
CONV_W = 4
RG_C = 8.0
H_C = 4
GATE_RANK = 16
GLA_TAU = 16.0
N_SUB = 3
LOG2E = 1.4426950408889634

LANES = 128
SUBLANES = 8
VMEM_LIMIT_BYTES = 56 * 1024 * 1024

REC_CHUNK = 128
MIX_TILE = 256
FFN_TILE = 512
DEC_GROUP = 8


def _dot(a, b):
    return jnp.dot(a, b, preferred_element_type=F32)


def _dot_nt(a, b):
    return lax.dot_general(a, b, (((1,), (1,)), ((), ())), preferred_element_type=F32)


def _dot_tn(a, b):
    return lax.dot_general(a, b, (((0,), (0,)), ((), ())), preferred_element_type=F32)


def _sigmoid(x):
    return 0.5 * jnp.tanh(0.5 * x) + 0.5


def _silu(x):
    h = 0.5 * x
    return h * jnp.tanh(h) + h


def _softplus(x):
    return jnp.maximum(x, 0.0) + jnp.log(1.0 + jnp.exp(-jnp.abs(x)))


def _log_sigmoid(x):
    return -_softplus(-x)


def _gelu_tanh(x):
    c = math.sqrt(2.0 / math.pi)
    return x * (0.5 * (1.0 + jnp.tanh(c * (x + 0.044715 * (x * x * x)))))


def _rms(x, gain):
    ms = jnp.mean(x * x, axis=-1, keepdims=True)
    return x * lax.rsqrt(ms + EPS) * gain


def _ada_norm(x, gain, shift, scale):
    return _rms(x, gain) * (1.0 + scale) + shift


def _ada_kernel(c_ref, w_ref, b_ref, o_ref):
    o_ref[...] = _dot(c_ref[...].astype(BF16), w_ref[...].astype(BF16)) + b_ref[...]


def _ada_call(c_all, w_ada, b_ada):
    depth, d, nm = w_ada.shape
    rows = c_all.shape[0]
    tn = 1536
    assert nm % tn == 0
    return pl.pallas_call(
        _ada_kernel,
        grid=(depth, nm // tn),
        in_specs=[
            pl.BlockSpec((rows, d), lambda l, n: (0, 0)),
            pl.BlockSpec((None, d, tn), lambda l, n: (l, 0, n)),
            pl.BlockSpec((None, 1, tn), lambda l, n: (l, 0, n)),
        ],
        out_specs=pl.BlockSpec((None, rows, tn), lambda l, n: (l, 0, n)),
        out_shape=jax.ShapeDtypeStruct((depth, rows, nm), F32),
        compiler_params=pltpu.CompilerParams(
            dimension_semantics=("arbitrary", "arbitrary"),
            vmem_limit_bytes=VMEM_LIMIT_BYTES),
        name="ada_mod",
    )(c_all, w_ada, b_ada.reshape(depth, 1, nm))


def _ffn_kernel(x_ref, sh_ref, sc_ref, gt_ref, gain_ref, wi_ref, wo_ref, *rest, n_chunks, final):
    if final:
        fg_ref, o_ref = rest
    else:
        (o_ref,) = rest
    x = x_ref[...]
    h = _ada_norm(x, gain_ref[...], sh_ref[0], sc_ref[0]).astype(BF16)
    f = wo_ref.shape[0]
    fc = f // n_chunks
    acc = None
    for c in range(n_chunks):
        g = _dot(h, wi_ref[:, c * fc:(c + 1) * fc])
        u = _dot(h, wi_ref[:, f + c * fc:f + (c + 1) * fc])
        a = (_silu(g) * u).astype(BF16)
        part = _dot(a, wo_ref[c * fc:(c + 1) * fc, :])
        acc = part if acc is None else acc + part
    y = x + (0.5 * gt_ref[0]) * acc
    if final:
        y = _rms(y, fg_ref[...])
    o_ref[...] = y


def _mod_spec(rows_per_group, d, layer, col, group_of):
    return pl.BlockSpec((None, 1, rows_per_group, d), lambda i: (layer, group_of(i), 0, col))


def _ffn_call(x, mod4, gain, wi, wo, layer, which, sub, tile, tiles_per_group, final_gain=None):
    n, d = x.shape
    f = wo.shape[2]
    r = mod4.shape[2]
    group_of = (lambda i: i // tiles_per_group)
    final = final_gain is not None
    in_specs = [
        pl.BlockSpec((tile, d), lambda i: (i, 0)),
        _mod_spec(r, d, layer, 3 * sub + 0, group_of),
        _mod_spec(r, d, layer, 3 * sub + 1, group_of),
        _mod_spec(r, d, layer, 3 * sub + 2, group_of),
        pl.BlockSpec((None, None, 1, d), lambda i: (layer, sub, 0, 0)),
        pl.BlockSpec((None, None, d, 2 * f), lambda i: (layer, which, 0, 0),
                     pipeline_mode=pl.Buffered(1)),
        pl.BlockSpec((None, None, f, d), lambda i: (layer, which, 0, 0),
                     pipeline_mode=pl.Buffered(1)),
    ]
    args = [x, mod4, mod4, mod4, gain, wi, wo]
    if final:
        in_specs.append(pl.BlockSpec((1, d), lambda i: (0, 0)))
        args.append(final_gain.reshape(1, d))
    return pl.pallas_call(
        functools.partial(_ffn_kernel, n_chunks=1, final=final),
        grid=(n // tile,),
        in_specs=in_specs,
        out_specs=pl.BlockSpec((tile, d), lambda i: (i, 0)),
        out_shape=jax.ShapeDtypeStruct((n, d), F32),
        compiler_params=pltpu.CompilerParams(
            dimension_semantics=("arbitrary",), vmem_limit_bytes=VMEM_LIMIT_BYTES),
        name="ffn",
    )(*args)


def _hgrn_gates(a_q, a_f, loglb, log1mlb, omlb):
    q = _silu(a_q)
    c = log1mlb + _log_sigmoid(a_f)
    m = jnp.maximum(loglb, c)
    log_f = m + jnp.log(1.0 + jnp.exp(-jnp.abs(loglb - c)))
    k = omlb * _sigmoid(-a_f)
    return q, k, log_f


def _gla_log_f(c_a, wal_ref, bal):
    return _log_sigmoid(_dot(c_a.astype(BF16), wal_ref[...]) + bal) * (1.0 / GLA_TAU)


def _rglru_gates(conv, wa_ref, wx_ref, ba, bx, lam):
    cb = conv.astype(BF16)
    bs = conv.shape[1] // NB_B
    r = jnp.concatenate([_dot(cb[:, n * bs:(n + 1) * bs], wa_ref[n]) for n in range(NB_B)], axis=-1)
    i = jnp.concatenate([_dot(cb[:, n * bs:(n + 1) * bs], wx_ref[n]) for n in range(NB_B)], axis=-1)
    r = _sigmoid(r + ba)
    i = _sigmoid(i + bx)
    log_a = (-RG_C) * r * _softplus(-lam)
    a = jnp.exp(log_a)
    u = jnp.sqrt(1.0 - jnp.exp(2.0 * log_a)) * i * conv
    return a, u


def _head_out(o_heads, gain, gate):
    return jnp.concatenate([_rms(o, gain) for o in o_heads], axis=-1) * _silu(gate)


def _rot_gates(zt, d):
    lane = lax.broadcasted_iota(jnp.int32, (zt.shape[0], LANES), 1)
    out = []
    for i in range(3):
        first = jnp.where(lane < GATE_RANK, zt[:, (i + 1) * d:(i + 1) * d + LANES],
                          zt[:, i * d:i * d + LANES])
        out.append(jnp.concatenate([first, zt[:, i * d + LANES:(i + 1) * d]], axis=1))
    return out


def _merge(x, gate, gates, y_a, y_b, y_c, wbr_ref, wout_ref):
    merged = (_sigmoid(gates[0]) * _dot(y_a.astype(BF16), wbr_ref[0])
              + _sigmoid(gates[1]) * _dot(y_b.astype(BF16), wbr_ref[1])
              + _sigmoid(gates[2]) * _dot(y_c.astype(BF16), wbr_ref[2]))
    return x + gate * _dot(merged.astype(BF16), wout_ref[...])


def _no_pump(count=1):
    del count


def _split3(x):
    hi = x.astype(BF16)
    r1 = x - hi.astype(F32)
    mid = r1.astype(BF16)
    lo = (r1 - mid.astype(F32)).astype(BF16)
    return hi, mid, lo


def _chunk_cumsum(g, chunk, tri):
    n = g.shape[0]
    hi, mid, lo = _split3(g)
    out = []
    for c in range(n // chunk):
        rs = slice(c * chunk, (c + 1) * chunk)
        out.append(_dot(tri, lo[rs]) + _dot(tri, mid[rs]) + _dot(tri, hi[rs]))
    return jnp.concatenate(out, axis=0) if len(out) > 1 else out[0]


def _level_operands(q, k, g, chunk, tri, pump=_no_pump):
    n, w = g.shape
    g = g * LOG2E
    rin = lax.broadcasted_iota(jnp.int32, (n, w), 0) & (chunk - 1)
    b = _chunk_cumsum(g, chunk, tri)
    xs = []
    n_levels = chunk.bit_length() - 1
    for l in range(n_levels):
        m = 1 << l
        right = (rin & m) != 0
        if l == 0:
            dlt = jnp.where(right, g, 0.0)
        elif l == 1:
            gm1 = pltpu.roll(g, 1, 0)
            gp1 = pltpu.roll(g, n - 1, 0)
            r4 = rin & 3
            dlt = jnp.where(r4 == 0, gp1, jnp.where(r4 == 1, 0.0, jnp.where(r4 == 2, g, g + gm1)))
        else:
            p = 2 * m
            b3 = b.reshape(n // p, p, w)
            ref = jnp.broadcast_to(b3[:, m - 1:m, :], (n // p, p, w)).reshape(n, w)
            dlt = -jnp.abs(b - ref)
        xs.append((jnp.where(right, q, k) * jnp.exp2(dlt)).astype(BF16))
        pump()
    b3 = b.reshape(n // chunk, chunk, w)
    bend3 = b3[:, chunk - 1:chunk, :]
    bend = jnp.broadcast_to(bend3, (n // chunk, chunk, w)).reshape(n, w)
    qe = (q * jnp.exp2(b)).astype(BF16)
    kd = (k * jnp.exp2(bend - b)).astype(BF16)
    ebend = jnp.exp2(bend3)
    return qe, kd, ebend, xs


def _level_masks(chunk):
    ri = lax.broadcasted_iota(jnp.int32, (chunk, chunk), 0)
    ci = lax.broadcasted_iota(jnp.int32, (chunk, chunk), 1)
    masks = []
    for l in range(chunk.bit_length() - 1):
        m = 1 << l
        same_parent = (ri >> (l + 1)) == (ci >> (l + 1))
        masks.append(same_parent & ((ri & m) != 0) & ((ci & m) == 0))
    tri = jnp.where(ri >= ci, 1.0, 0.0).astype(BF16)
    return masks, ri == ci, tri


def _recurrence_tile(q, k, v, g, st_ref, keep, n_heads, chunk, masks, eye, tri, pump=_no_pump):
    n = q.shape[0]
    dk = q.shape[1] // n_heads
    dv = v.shape[1] // n_heads
    qe, kd, ebend, xs = _level_operands(q, k, g, chunk, tri, pump)
    vb = v.astype(BF16)
    qk = q * k
    outs = []
    for h in range(n_heads):
        ks = slice(h * dk, (h + 1) * dk)
        vs = slice(h * dv, (h + 1) * dv)
        st = st_ref[h] * keep
        o_chunks = []
        for c in range(n // chunk):
            rs = slice(c * chunk, (c + 1) * chunk)
            dq = jnp.sum(qk[rs, ks], axis=-1, keepdims=True)
            scores = jnp.where(eye, dq, 0.0)
            for l, x in enumerate(xs):
                xl = x[rs, ks]
                scores = jnp.where(masks[l], _dot_nt(xl, xl), scores)
            vc = vb[rs, vs]
            o = _dot_nt(qe[rs, ks], st.astype(BF16)) + _dot(scores.astype(BF16), vc)
            o_chunks.append(o)
            st = st * ebend[c][:, ks] + _dot_tn(vc, kd[rs, ks])
        st_ref[h] = st
        outs.append(jnp.concatenate(o_chunks, axis=0) if len(o_chunks) > 1 else o_chunks[0])
    return outs


def _compose_scan(a, u, idx, axis, length, pump):
    s = 1
    while s < length:
        valid = idx >= s
        a_sh = pltpu.roll(a, s, axis)
        u_sh = pltpu.roll(u, s, axis)
        u = jnp.where(valid, a * u_sh + u, u)
        a = jnp.where(valid, a * a_sh, a)
        pump()
        s *= 2
    return a, u


def _linear_scan(a, u, h0, pump=_no_pump):
    n = a.shape[0]
    sub = lax.broadcasted_iota(jnp.int32, a.shape, 0) & (SUBLANES - 1)
    a_loc, u_loc = _compose_scan(a, u, sub, 0, SUBLANES, pump)
    groups = n // SUBLANES
    hs = []
    h_prev = h0
    for g in range(groups):
        rs = slice(g * SUBLANES, (g + 1) * SUBLANES)
        hg = u_loc[rs] + a_loc[rs] * h_prev
        hs.append(hg)
        h_prev = hg[SUBLANES - 1:SUBLANES]
        if g % 4 == 3:
            pump()
    return jnp.concatenate(hs, axis=0), h_prev


def _mixer_kernel(xc_ref, xp_ref, sh_ref, sc_ref, gt_ref, gain_ref, win_ref, wtl_ref, lbp_ref, hn_ref,
                  cw_ref, rgp_ref, rwa_ref, rwx_ref, wal_ref, bal_ref, gn_ref,
                  wbr_ref, wout_ref,
                  o_ref, sa_ref, cv_ref, hr_ref, sg_ref,
                  z0, z1, sta, stc, cvs, hrs, *, dims, tpb):
    d_a, d_b, dk_tot, d_c = dims
    s = pl.program_id(0)
    n, d = xc_ref.shape
    o_b = 4 * d_a
    o_c = o_b + 2 * d_b
    o_t = o_c + 2 * dk_tot + 2 * d_c
    o_end = o_t + 3 * d
    dk_c = dk_tot // H_C
    prev = s - 1

    @pl.when(s == 0)
    def _():
        z1[...] = jnp.zeros_like(z1)
        sta[...] = jnp.zeros_like(sta)
        stc[...] = jnp.zeros_like(stc)
        cvs[...] = jnp.zeros_like(cvs)
        hrs[...] = jnp.zeros_like(hrs)

    def step(zw, zr):
        h = _ada_norm(xc_ref[...], gain_ref[...], sh_ref[0], sc_ref[0]).astype(BF16)
        piece = 2 * LANES
        todo = list(range(0, o_end, piece)) + [o_end]

        def pump(count=1):
            for _ in range(count):
                if todo:
                    c0 = todo.pop(0)
                    if c0 < o_end:
                        zw[:, c0:c0 + piece] = _dot(h, win_ref[:, c0:c0 + piece])
                    else:
                        zw[:, c0:c0 + LANES] = _dot(h, wtl_ref[...])

        keep = jnp.where(lax.rem(prev, tpb) == 0, 0.0, 1.0).astype(F32)
        masks, eye, tri = _level_masks(REC_CHUNK)
        x = xp_ref[...]

        lbp = lbp_ref[...]
        pump(2)
        q, k, log_f = _hgrn_gates(zr[:, 0:d_a], zr[:, d_a:2 * d_a], lbp[0:1], lbp[1:2], lbp[2:3])
        pump(2)
        oa = _recurrence_tile(q, k, zr[:, 2 * d_a:3 * d_a], log_f, sta, keep,
                              H_A, REC_CHUNK, masks, eye, tri, pump)
        y_a = _head_out(oa, hn_ref[...], zr[:, 3 * d_a:4 * d_a])
        pump(2)

        bx = zr[:, o_b:o_b + d_b]
        cw = cw_ref[...]
        rgp = rgp_ref[...]
        ext = jnp.concatenate([cvs[...] * keep, bx], axis=0)
        conv = rgp[0:1] + cw[3:4] * bx
        for jj in range(CONV_W - 1):
            off = SUBLANES - (CONV_W - 1) + jj
            conv = conv + cw[jj:jj + 1] * ext[off:off + n]
        cvs[...] = bx[n - SUBLANES:n]
        pump(2)
        a, u = _rglru_gates(conv, rwa_ref, rwx_ref, rgp[1:2], rgp[2:3], rgp[3:4])
        pump(2)
        hseq, hlast = _linear_scan(a, u, hrs[0:1] * keep, pump)
        hrs[...] = jnp.broadcast_to(hlast, hrs.shape)
        y_b = _gelu_tanh(zr[:, o_b + d_b:o_c]) * hseq

        log_fc = _gla_log_f(zr[:, o_t:o_t + LANES], wal_ref, bal_ref[...])
        oc = _recurrence_tile(zr[:, o_c:o_c + dk_tot] * (dk_c ** -0.5),
                              zr[:, o_c + dk_tot:o_c + 2 * dk_tot],
                              zr[:, o_c + 2 * dk_tot:o_c + 2 * dk_tot + d_c], log_fc, stc, keep,
                              H_C, REC_CHUNK, masks, eye, tri, pump)
        y_c = _head_out(oc, gn_ref[...], zr[:, o_c + 2 * dk_tot + d_c:o_t])
        pump(len(todo))

        gates = _rot_gates(zr[:, o_t:o_end + LANES], d)
        o_ref[...] = _merge(x, gt_ref[0], gates, y_a, y_b, y_c, wbr_ref, wout_ref)

    @pl.when(lax.rem(s, 2) == 0)
    def _():
        step(z0, z1)

    @pl.when(lax.rem(s, 2) == 1)
    def _():
        step(z1, z0)

    @pl.when(jnp.logical_and(s >= 1, lax.rem(prev, tpb) == tpb - 1))
    def _():
        for hh in range(H_A):
            sa_ref[hh] = sta[hh].T
        for hp in range(H_C // 2):
            pair = jnp.concatenate([stc[2 * hp], stc[2 * hp + 1]], axis=1).T
            sg_ref[2 * hp] = pair[0:dk_c]
            sg_ref[2 * hp + 1] = pair[dk_c:2 * dk_c]
        cv_ref[...] = cvs[SUBLANES - (CONV_W - 1):SUBLANES]
        hr_ref[...] = hrs[0:1]


def _mixer_call(x, mod4, gain, win, lbp, hnorm, cw, rgp, rwa, rwx, wal, bal, gnorm, wbr, wout,
                layer, batch, seq, dims):
    n, d = x.shape
    d_a, d_b, dk_tot, d_c = dims
    tile = min(MIX_TILE, seq)
    assert seq % tile == 0 and tile % REC_CHUNK == 0
    tpb = seq // tile
    dk_a, dv_a = d_a // H_A, d_a // H_A
    dk_c, dv_c = dk_tot // H_C, d_c // H_C
    win_main, win_tail = win
    nmain = 4 * d_a + 2 * d_b + 2 * dk_tot + 2 * d_c + 3 * d
    assert win_main.shape[2] == nmain + GATE_RANK and nmain % (2 * LANES) == 0
    nin = nmain + LANES
    nt = batch * tpb
    cur = lambda s: jnp.minimum(s, nt - 1)
    prv = lambda s: jnp.maximum(s - 1, 0)
    mspec = lambda col, t: pl.BlockSpec((None, 1, 1, d), lambda s: (layer, t(s) // tpb, 0, col))
    lsel = lambda *rest: (lambda s: (layer,) + rest)
    in_specs = [
        pl.BlockSpec((tile, d), lambda s: (cur(s), 0)),
        pl.BlockSpec((tile, d), lambda s: (prv(s), 0)),
        mspec(3, cur), mspec(4, cur), mspec(5, prv),
        pl.BlockSpec((None, None, 1, d), lsel(1, 0, 0)),
        pl.BlockSpec((None, d, nmain), lsel(0, 0), pipeline_mode=pl.Buffered(1)),
        pl.BlockSpec((None, d, LANES), lsel(0, 0)),
        pl.BlockSpec((None, 3, d_a), lsel(0, 0)),
        pl.BlockSpec((None, 1, dv_a), lsel(0, 0)),
        pl.BlockSpec((None, CONV_W, d_b), lsel(0, 0)),
        pl.BlockSpec((None, 4, d_b), lsel(0, 0)),
        pl.BlockSpec((None, NB_B, d_b // NB_B, d_b // NB_B), lsel(0, 0, 0)),
        pl.BlockSpec((None, NB_B, d_b // NB_B, d_b // NB_B), lsel(0, 0, 0)),
        pl.BlockSpec((None, LANES, dk_tot), lsel(0, 0)),
        pl.BlockSpec((None, 1, dk_tot), lsel(0, 0)),
        pl.BlockSpec((None, 1, dv_c), lsel(0, 0)),
        pl.BlockSpec((None, 3, d_a, d), lsel(0, 0, 0), pipeline_mode=pl.Buffered(1)),
        pl.BlockSpec((None, d, d), lsel(0, 0), pipeline_mode=pl.Buffered(1)),
    ]
    out_specs = [
        pl.BlockSpec((tile, d), lambda s: (prv(s), 0)),
        pl.BlockSpec((None, H_A, dk_a, dv_a), lambda s: (prv(s) // tpb, 0, 0, 0)),
        pl.BlockSpec((None, CONV_W - 1, d_b), lambda s: (prv(s) // tpb, 0, 0)),
        pl.BlockSpec((None, 1, d_b), lambda s: (prv(s) // tpb, 0, 0)),
        pl.BlockSpec((None, H_C, dk_c, dv_c), lambda s: (prv(s) // tpb, 0, 0, 0)),
    ]
    out_shape = [
        jax.ShapeDtypeStruct((n, d), F32),
        jax.ShapeDtypeStruct((batch, H_A, dk_a, dv_a), F32),
        jax.ShapeDtypeStruct((batch, CONV_W - 1, d_b), F32),
        jax.ShapeDtypeStruct((batch, 1, d_b), F32),
        jax.ShapeDtypeStruct((batch, H_C, dk_c, dv_c), F32),
    ]
    scratch = [
        pltpu.VMEM((tile, nin), F32),
        pltpu.VMEM((tile, nin), F32),
        pltpu.VMEM((H_A, dv_a, dk_a), F32),
        pltpu.VMEM((H_C, dv_c, dk_c), F32),
        pltpu.VMEM((SUBLANES, d_b), F32),
        pltpu.VMEM((SUBLANES, d_b), F32),
    ]
    return pl.pallas_call(
        functools.partial(_mixer_kernel, dims=dims, tpb=tpb),
        grid=(nt + 1,),
        in_specs=in_specs,
        out_specs=out_specs,
        out_shape=out_shape,
        scratch_shapes=scratch,
        compiler_params=pltpu.CompilerParams(
            dimension_semantics=("arbitrary",), vmem_limit_bytes=VMEM_LIMIT_BYTES),
        name="mixer_prompt",
    )(x, x, mod4, mod4, mod4, gain, win_main, win_tail, lbp, hnorm, cw, rgp, rwa, rwx, wal, bal,
      gnorm, wbr, wout)


def _dec_pre_kernel(x_ref, sh_ref, sc_ref, gain_ref, win_ref, wtl_ref, lbp_ref, cw_ref, rgp_ref,
                    rwa_ref, rwx_ref, wal_ref, bal_ref, cvin_ref, hrin_ref,
                    sv_ref, post_ref, cvout_ref, hrout_ref, *, dims):
    d_a, d_b, dk_tot, d_c = dims
    x = x_ref[...]
    d = x.shape[1]
    h = _ada_norm(x, gain_ref[...], sh_ref[0], sc_ref[0]).astype(BF16)
    o_b = 4 * d_a
    o_c = o_b + 2 * d_b
    o_t = o_c + 2 * dk_tot + 2 * d_c
    o_end = o_t + 3 * d

    za = _dot(h, win_ref[:, 0:o_b])
    lbp = lbp_ref[...]
    q, k, log_f = _hgrn_gates(za[:, 0:d_a], za[:, d_a:2 * d_a], lbp[0:1], lbp[1:2], lbp[2:3])

    zb = _dot(h, win_ref[:, o_b:o_c])
    bx = zb[:, 0:d_b]
    cw = cw_ref[...]
    rgp = rgp_ref[...]
    cvin = cvin_ref[...]
    conv = rgp[0:1] + cw[3:4] * bx
    for jj in range(CONV_W - 1):
        conv = conv + cw[jj:jj + 1] * cvin[:, jj * d_b:(jj + 1) * d_b]
    a, u = _rglru_gates(conv, rwa_ref, rwx_ref, rgp[1:2], rgp[2:3], rgp[3:4])
    hnew = u + a * hrin_ref[...]
    y_b = _gelu_tanh(zb[:, d_b:2 * d_b]) * hnew
    cvout_ref[...] = jnp.concatenate([cvin[:, d_b:(CONV_W - 1) * d_b], bx], axis=-1)
    hrout_ref[...] = hnew

    zc = _dot(h, win_ref[:, o_c:o_t])
    zt = jnp.concatenate([_dot(h, win_ref[:, o_t:o_end]), _dot(h, wtl_ref[...])], axis=-1)
    log_fc = _gla_log_f(zt[:, 0:LANES], wal_ref, bal_ref[...])
    dk_c = dk_tot // H_C

    sv_ref[...] = jnp.concatenate(
        [q, jnp.exp(log_f), k, za[:, 2 * d_a:3 * d_a],
         zc[:, 0:dk_tot] * (dk_c ** -0.5), jnp.exp(log_fc), zc[:, dk_tot:2 * dk_tot],
         zc[:, 2 * dk_tot:2 * dk_tot + d_c]], axis=-1)
    post_ref[...] = jnp.concatenate(
        [za[:, 3 * d_a:4 * d_a], y_b, zc[:, 2 * dk_tot + d_c:2 * dk_tot + 2 * d_c]]
        + _rot_gates(zt, d), axis=-1)


def _dec_pre_call(x, mod4, gain, win, lbp, cw, rgp, rwa, rwx, wal, bal, cv_all, hr_all, layer, dims):
    n, d = x.shape
    d_a, d_b, dk_tot, d_c = dims
    win_main, win_tail = win
    nmain = win_main.shape[2] - GATE_RANK
    sv_w = 4 * d_a + 3 * dk_tot + d_c
    post_w = d_a + d_b + d_c + 3 * d
    mspec = lambda col: pl.BlockSpec((None, 1, n, d), lambda i: (layer, 0, 0, col))
    lsel = lambda *rest: (lambda i: (layer,) + rest)
    in_specs = [
        pl.BlockSpec((n, d), lambda i: (0, 0)),
        mspec(3), mspec(4),
        pl.BlockSpec((None, None, 1, d), lsel(1, 0, 0)),
        pl.BlockSpec((None, d, nmain), lsel(0, 0), pipeline_mode=pl.Buffered(1)),
        pl.BlockSpec((None, d, LANES), lsel(0, 0)),
        pl.BlockSpec((None, 3, d_a), lsel(0, 0)),
        pl.BlockSpec((None, CONV_W, d_b), lsel(0, 0)),
        pl.BlockSpec((None, 4, d_b), lsel(0, 0)),
        pl.BlockSpec((None, NB_B, d_b // NB_B, d_b // NB_B), lsel(0, 0, 0)),
        pl.BlockSpec((None, NB_B, d_b // NB_B, d_b // NB_B), lsel(0, 0, 0)),
        pl.BlockSpec((None, LANES, dk_tot), lsel(0, 0)),
        pl.BlockSpec((None, 1, dk_tot), lsel(0, 0)),
        pl.BlockSpec((None, n, (CONV_W - 1) * d_b), lsel(0, 0)),
        pl.BlockSpec((None, n, d_b), lsel(0, 0)),
    ]
    full = lambda w: pl.BlockSpec((n, w), lambda i: (0, 0))
    return pl.pallas_call(
        functools.partial(_dec_pre_kernel, dims=dims),
        grid=(1,),
        in_specs=in_specs,
        out_specs=[full(sv_w), full(post_w), full((CONV_W - 1) * d_b), full(d_b)],
        out_shape=[jax.ShapeDtypeStruct((n, sv_w), F32), jax.ShapeDtypeStruct((n, post_w), F32),
                   jax.ShapeDtypeStruct((n, (CONV_W - 1) * d_b), F32),
                   jax.ShapeDtypeStruct((n, d_b), F32)],
        compiler_params=pltpu.CompilerParams(
            dimension_semantics=("arbitrary",), vmem_limit_bytes=VMEM_LIMIT_BYTES),
        name="mixer_sample_pre",
    )(x, mod4, mod4, gain, win_main, win_tail, lbp, cw, rgp, rwa, rwx, wal, bal, cv_all, hr_all)


def _columns(rows):
    pad = jnp.zeros((LANES - rows.shape[0], LANES), F32)
    return jnp.concatenate([rows, pad], axis=0).T


def _dec_state_kernel(sv_ref, sa_ref, sg_ref, *rest, dims):
    oa_ref, oc_ref, sa_out, sg_out = rest[-4:]
    d_a, d_b, dk_tot, d_c = dims
    sv = sv_ref[...]
    g = sv.shape[0]
    dk_a = d_a // H_A
    dk_c = dk_tot // H_C
    dv_c = d_c // H_C
    o_q, o_f, o_k, o_v = 0, d_a, 2 * d_a, 3 * d_a
    for hh in range(H_A):
        qt = _columns(sv[:, o_q + hh * dk_a:o_q + (hh + 1) * dk_a])
        ft = _columns(sv[:, o_f + hh * dk_a:o_f + (hh + 1) * dk_a])
        kt = _columns(sv[:, o_k + hh * dk_a:o_k + (hh + 1) * dk_a])
        for jj in range(g):
            v = sv[jj:jj + 1, o_v + hh * dk_a:o_v + (hh + 1) * dk_a]
            s_new = ft[:, jj:jj + 1] * sa_ref[jj, hh] + kt[:, jj:jj + 1] * v
            sa_out[jj, hh] = s_new
            oa_ref[jj:jj + 1, hh * dk_a:(hh + 1) * dk_a] = jnp.sum(
                qt[:, jj:jj + 1] * s_new, axis=0, keepdims=True)
    c_q = 4 * d_a
    c_f = c_q + dk_tot
    c_k = c_f + dk_tot
    c_v = c_k + dk_tot
    per_tile = LANES // dk_c
    for tp in range(dk_tot // LANES):
        qt = _columns(sv[:, c_q + tp * LANES:c_q + (tp + 1) * LANES])
        ft = _columns(sv[:, c_f + tp * LANES:c_f + (tp + 1) * LANES])
        kt = _columns(sv[:, c_k + tp * LANES:c_k + (tp + 1) * LANES])
        for hp in range(per_tile):
            hh = tp * per_tile + hp
            rs = slice(hp * dk_c, (hp + 1) * dk_c)
            for jj in range(g):
                v = sv[jj:jj + 1, c_v + hh * dv_c:c_v + (hh + 1) * dv_c]
                s_new = ft[rs, jj:jj + 1] * sg_ref[jj, hh] + kt[rs, jj:jj + 1] * v
                sg_out[jj, hh] = s_new
                oc_ref[jj:jj + 1, hh * dv_c:(hh + 1) * dv_c] = jnp.sum(
                    qt[rs, jj:jj + 1] * s_new, axis=0, keepdims=True)


def _dec_state_call(sv, st_a, st_c, layer, dims, stacked=None):
    n, sv_w = sv.shape
    d_a, d_b, dk_tot, d_c = dims
    depth = st_a.shape[0]
    g = DEC_GROUP
    assert n % g == 0
    dk_a = d_a // H_A
    dk_c, dv_c = dk_tot // H_C, d_c // H_C
    in_specs = [
        pl.BlockSpec((g, sv_w), lambda i: (i, 0)),
        pl.BlockSpec((None, g, H_A, dk_a, dk_a), lambda i: (layer, i, 0, 0, 0)),
        pl.BlockSpec((None, g, H_C, dk_c, dv_c), lambda i: (layer, i, 0, 0, 0)),
    ]
    args = [sv, st_a, st_c]
    aliases = {}
    if stacked is not None:
        in_specs += [pl.BlockSpec(memory_space=pl.ANY), pl.BlockSpec(memory_space=pl.ANY)]
        args += list(stacked)
        aliases = {3: 2, 4: 3}
    return pl.pallas_call(
        functools.partial(_dec_state_kernel, dims=dims),
        grid=(n // g,),
        in_specs=in_specs,
        out_specs=[
            pl.BlockSpec((g, d_a), lambda i: (i, 0)),
            pl.BlockSpec((g, d_c), lambda i: (i, 0)),
            pl.BlockSpec((None, g, H_A, dk_a, dk_a), lambda i: (layer, i, 0, 0, 0)),
            pl.BlockSpec((None, g, H_C, dk_c, dv_c), lambda i: (layer, i, 0, 0, 0)),
        ],
        out_shape=[
            jax.ShapeDtypeStruct((n, d_a), F32),
            jax.ShapeDtypeStruct((n, d_c), F32),
            jax.ShapeDtypeStruct((depth, n, H_A, dk_a, dk_a), F32),
            jax.ShapeDtypeStruct((depth, n, H_C, dk_c, dv_c), F32),
        ],
        input_output_aliases=aliases,
        compiler_params=pltpu.CompilerParams(
            dimension_semantics=("arbitrary",), vmem_limit_bytes=VMEM_LIMIT_BYTES),
        name="mixer_sample_state",
    )(*args)


def _dec_post_kernel(x_ref, gt_ref, post_ref, oa_ref, oc_ref, hn_ref, gn_ref, wbr_ref, wout_ref,
                     o_ref, *, dims):
    d_a, d_b, dk_tot, d_c = dims
    x = x_ref[...]
    post = post_ref[...]
    oa = oa_ref[...]
    oc = oc_ref[...]
    dv_a = d_a // H_A
    dv_c = d_c // H_C
    y_a = _head_out([oa[:, hh * dv_a:(hh + 1) * dv_a] for hh in range(H_A)], hn_ref[...],
                    post[:, 0:d_a])
    y_b = post[:, d_a:d_a + d_b]
    y_c = _head_out([oc[:, hh * dv_c:(hh + 1) * dv_c] for hh in range(H_C)], gn_ref[...],
                    post[:, d_a + d_b:d_a + d_b + d_c])
    off = d_a + d_b + d_c
    d = x.shape[1]
    gates = [post[:, off + i * d:off + (i + 1) * d] for i in range(3)]
    o_ref[...] = _merge(x, gt_ref[0], gates, y_a, y_b, y_c, wbr_ref, wout_ref)


def _dec_post_call(x, mod4, post, oa, oc, hnorm, gnorm, wbr, wout, layer, dims):
    n, d = x.shape
    d_a, d_b, dk_tot, d_c = dims
    lsel = lambda *rest: (lambda i: (layer,) + rest)
    full = lambda w: pl.BlockSpec((n, w), lambda i: (0, 0))
    return pl.pallas_call(
        functools.partial(_dec_post_kernel, dims=dims),
        grid=(1,),
        in_specs=[
            full(d),
            pl.BlockSpec((None, 1, n, d), lambda i: (layer, 0, 0, 5)),
            full(post.shape[1]), full(d_a), full(d_c),
            pl.BlockSpec((None, 1, d_a // H_A), lsel(0, 0)),
            pl.BlockSpec((None, 1, d_c // H_C), lsel(0, 0)),
            pl.BlockSpec((None, 3, d_a, d), lsel(0, 0, 0), pipeline_mode=pl.Buffered(1)),
            pl.BlockSpec((None, d, d), lsel(0, 0), pipeline_mode=pl.Buffered(1)),
        ],
        out_specs=full(d),
        out_shape=jax.ShapeDtypeStruct((n, d), F32),
        compiler_params=pltpu.CompilerParams(
            dimension_semantics=("arbitrary",), vmem_limit_bytes=VMEM_LIMIT_BYTES),
        name="mixer_sample_post",
    )(x, mod4, post, oa, oc, hnorm, gnorm, wbr, wout)


def kernel(x_prompt, x_sample, c_prompt, c_sample, state_hgrn, state_conv, state_rglru, state_gla,
           w_ada, b_ada, norm_gain, w_ffn_in, w_ffn_out, w_in, hgrn_lb_logits, hgrn_norm,
           conv_w, conv_b, rg_wa, rg_ba, rg_wx, rg_bx, rg_lambda, gla_w_alpha, gla_b_alpha,
           gla_norm, w_br_a, w_br_b, w_br_c, w_out, final_norm):
    batch, seq, d = x_prompt.shape
    nd = x_sample.shape[0]
    depth = w_in.shape[0]
    d_a = hgrn_lb_logits.shape[1]
    d_b = conv_b.shape[1]
    dk_tot = gla_b_alpha.shape[1]
    d_c = w_br_c.shape[1]
    dims = (d_a, d_b, dk_tot, d_c)
    nm = w_ada.shape[2]

    wi = w_ffn_in.astype(BF16)
    wo = w_ffn_out.astype(BF16)
    o_ca = 4 * d_a + 2 * d_b + 2 * dk_tot + 2 * d_c
    nmain = w_in.shape[2] - GATE_RANK
    assert nmain == o_ca + 3 * d
    win = (w_in.astype(BF16),
           jnp.pad(w_in[:, :, nmain:], ((0, 0), (0, 0), (0, LANES - GATE_RANK))).astype(BF16))
    wal = jnp.concatenate(
        [gla_w_alpha, jnp.zeros((depth, LANES - GATE_RANK, dk_tot), gla_w_alpha.dtype)],
        axis=1).astype(BF16)
    bal = gla_b_alpha.reshape(depth, 1, dk_tot)
    wbr = jnp.roll(jnp.stack([w_br_a, w_br_b, w_br_c], axis=1), GATE_RANK, axis=-1).astype(BF16)
    wout = jnp.roll(w_out, GATE_RANK, axis=1).astype(BF16)
    rwa = rg_wa.astype(BF16)
    rwx = rg_wx.astype(BF16)
    rgp = jnp.stack([conv_b, rg_ba, rg_bx, rg_lambda], axis=1)
    hnorm = hgrn_norm.reshape(depth, 1, -1)
    gnorm = gla_norm.reshape(depth, 1, -1)
    lb = jnp.cumsum(jax.nn.softmax(hgrn_lb_logits.astype(F32), axis=0), axis=0)
    lb = lb - lb[0]
    lbp = jnp.stack([jnp.log(lb), jnp.log1p(-lb), 1.0 - lb], axis=1)

    norm_gain = norm_gain.reshape(depth, N_SUB, 1, d)
    c_all = jnp.concatenate([c_prompt, c_sample], axis=0)
    mod = _ada_call(c_all, w_ada, b_ada)
    mod_p = mod[:, :batch].reshape(depth, batch, 1, nm)
    mod_s = mod[:, batch:].reshape(depth, 1, nd, nm)

    ffn_tile = min(FFN_TILE, seq)
    tpg = seq // ffn_tile
    x = x_prompt.reshape(batch * seq, d)
    sa_p, cv_p, hr_p, sg_p = [], [], [], []
    for l in range(depth):
        x = _ffn_call(x, mod_p, norm_gain, wi, wo, l, 0, 0, ffn_tile, tpg)
        x, sa, cv, hr, sg = _mixer_call(x, mod_p, norm_gain, win, lbp, hnorm, conv_w, rgp, rwa, rwx,
                                        wal, bal, gnorm, wbr, wout, l, batch, seq, dims)
        x = _ffn_call(x, mod_p, norm_gain, wi, wo, l, 1, 2, ffn_tile, tpg,
                      final_gain=final_norm if l == depth - 1 else None)
        sa_p.append(sa); cv_p.append(cv); hr_p.append(hr[:, 0]); sg_p.append(sg)
    y_prompt = x.reshape(batch, seq, d)

    xs = x_sample.reshape(nd, d)
    cv_all = state_conv.reshape(depth, nd, (CONV_W - 1) * d_b)
    cv_s, hr_s = [], []
    stacked = None
    for l in range(depth):
        xs = _ffn_call(xs, mod_s, norm_gain, wi, wo, l, 0, 0, nd, 1)
        sv, post, cv, hr = _dec_pre_call(xs, mod_s, norm_gain, win, lbp, conv_w, rgp, rwa, rwx,
                                         wal, bal, cv_all, state_rglru, l, dims)
        oa, oc, sa_s, sg_s = _dec_state_call(sv, state_hgrn, state_gla, l, dims, stacked)
        stacked = (sa_s, sg_s)
        xs = _dec_post_call(xs, mod_s, post, oa, oc, hnorm, gnorm, wbr, wout, l, dims)
        xs = _ffn_call(xs, mod_s, norm_gain, wi, wo, l, 1, 2, nd, 1,
                       final_gain=final_norm if l == depth - 1 else None)
        cv_s.append(cv.reshape(nd, CONV_W - 1, d_b)); hr_s.append(hr)
    y_sample = xs.reshape(nd, 1, d)

    st = jnp.stack
    return (y_prompt, y_sample, st(sa_p), sa_s, st(cv_p), st(cv_s), st(hr_p), st(hr_s),
            st(sg_p), sg_s)
```

```python
import functools
import math

import jax
import jax.numpy as jnp
from jax import lax
from jax.experimental import pallas as pl
from jax.experimental.pallas import tpu as pltpu

F32 = jnp.float32
BF16 = jnp.bfloat16

EPS = 1e-6
H_A = 4
NB_B = 4
CONV_W = 4
RG_C = 8.0
H_C = 4
GATE_RANK = 16
GLA_TAU = 16.0
N_SUB = 3
LOG2E = 1.4426950408889634

LANES = 128
SUBLANES = 8
VMEM_LIMIT_BYTES = 56 * 1024 * 1024

REC_CHUNK = 128
MIX_TILE = 256
FFN_TILE = 512
DEC_GROUP = 8


def _dot(a, b):
    return jnp.dot(a, b, preferred_element_type=F32)


def _dot_nt(a, b):
    return lax.dot_general(a, b, (((1,), (1,)), ((), ())), preferred_element_type=F32)


def _dot_tn(a, b):
    return lax.dot_general(a, b, (((0,), (0,)), ((), ())), preferred_element_type=F32)


def _sigmoid(x):
    return 0.5 * jnp.tanh(0.5 * x) + 0.5


def _silu(x):
    h = 0.5 * x
    return h * jnp.tanh(h) + h


def _softplus(x):
    return jnp.maximum(x, 0.0) + jnp.log(1.0 + jnp.exp(-jnp.abs(x)))


def _log_sigmoid(x):
    return -_softplus(-x)


def _gelu_tanh(x):
    c = math.sqrt(2.0 / math.pi)
    return x * (0.5 * (1.0 + jnp.tanh(c * (x + 0.044715 * (x * x * x)))))


def _rms(x, gain):
    ms = jnp.mean(x * x, axis=-1, keepdims=True)
    return x * lax.rsqrt(ms + EPS) * gain


def _ada_norm(x, gain, shift, scale):
    return _rms(x, gain) * (1.0 + scale) + shift


def _ada_kernel(c_ref, w_ref, b_ref, o_ref):
    o_ref[...] = _dot(c_ref[...].astype(BF16), w_ref[...].astype(BF16)) + b_ref[...]


def _ada_call(c_all, w_ada, b_ada):
    depth, d, nm = w_ada.shape
    rows = c_all.shape[0]
    tn = 1536
    assert nm % tn == 0
    return pl.pallas_call(
        _ada_kernel,
        grid=(depth, nm // tn),
        in_specs=[
            pl.BlockSpec((rows, d), lambda l, n: (0, 0)),
            pl.BlockSpec((None, d, tn), lambda l, n: (l, 0, n)),
            pl.BlockSpec((None, 1, tn), lambda l, n: (l, 0, n)),
        ],
        out_specs=pl.BlockSpec((None, rows, tn), lambda l, n: (l, 0, n)),
        out_shape=jax.ShapeDtypeStruct((depth, rows, nm), F32),
        compiler_params=pltpu.CompilerParams(
            dimension_semantics=("arbitrary", "arbitrary"),
            vmem_limit_bytes=VMEM_LIMIT_BYTES),
        name="ada_mod",
    )(c_all, w_ada, b_ada.reshape(depth, 1, nm))


def _ffn_kernel(x_ref, sh_ref, sc_ref, gt_ref, gain_ref, wi_ref, wo_ref, *rest, n_chunks, final):
    if final:
        fg_ref, o_ref = rest
    else:
        (o_ref,) = rest
    x = x_ref[...]
    h = _ada_norm(x, gain_ref[...], sh_ref[0], sc_ref[0]).astype(BF16)
    f = wo_ref.shape[0]
    fc = f // n_chunks
    acc = None
    for c in range(n_chunks):
        g = _dot(h, wi_ref[:, c * fc:(c + 1) * fc])
        u = _dot(h, wi_ref[:, f + c * fc:f + (c + 1) * fc])
        a = (_silu(g) * u).astype(BF16)
        part = _dot(a, wo_ref[c * fc:(c + 1) * fc, :])
        acc = part if acc is None else acc + part
    y = x + (0.5 * gt_ref[0]) * acc
    if final:
        y = _rms(y, fg_ref[...])
    o_ref[...] = y


def _mod_spec(rows_per_group, d, layer, col, group_of):
    return pl.BlockSpec((None, 1, rows_per_group, d), lambda i: (layer, group_of(i), 0, col))


def _ffn_call(x, mod4, gain, wi, wo, layer, which, sub, tile, tiles_per_group, final_gain=None):
    n, d = x.shape
    f = wo.shape[2]
    r = mod4.shape[2]
    group_of = (lambda i: i // tiles_per_group)
    final = final_gain is not None
    in_specs = [
        pl.BlockSpec((tile, d), lambda i: (i, 0)),
        _mod_spec(r, d, layer, 3 * sub + 0, group_of),
        _mod_spec(r, d, layer, 3 * sub + 1, group_of),
        _mod_spec(r, d, layer, 3 * sub + 2, group_of),
        pl.BlockSpec((None, None, 1, d), lambda i: (layer, sub, 0, 0)),
        pl.BlockSpec((None, None, d, 2 * f), lambda i: (layer, which, 0, 0),
                     pipeline_mode=pl.Buffered(1)),
        pl.BlockSpec((None, None, f, d), lambda i: (layer, which, 0, 0),
                     pipeline_mode=pl.Buffered(1)),
    ]
    args = [x, mod4, mod4, mod4, gain, wi, wo]
    if final:
        in_specs.append(pl.BlockSpec((1, d), lambda i: (0, 0)))
        args.append(final_gain.reshape(1, d))
    return pl.pallas_call(
        functools.partial(_ffn_kernel, n_chunks=1, final=final),
        grid=(n // tile,),
        in_specs=in_specs,
        out_specs=pl.BlockSpec((tile, d), lambda i: (i, 0)),
        out_shape=jax.ShapeDtypeStruct((n, d), F32),
        compiler_params=pltpu.CompilerParams(
            dimension_semantics=("arbitrary",), vmem_limit_bytes=VMEM_LIMIT_BYTES),
        name="ffn",
    )(*args)


def _hgrn_gates(a_q, a_f, loglb, log1mlb, omlb):
    q = _silu(a_q)
    c = log1mlb + _log_sigmoid(a_f)
    m = jnp.maximum(loglb, c)
    log_f = m + jnp.log(1.0 + jnp.exp(-jnp.abs(loglb - c)))
    k = omlb * _sigmoid(-a_f)
    return q, k, log_f


def _gla_log_f(c_a, wal_ref, bal):
    return _log_sigmoid(_dot(c_a.astype(BF16), wal_ref[...]) + bal) * (1.0 / GLA_TAU)


def _rglru_gates(conv, wax_ref, ba, bx, lam):
    cb = conv.astype(BF16)
    bs = conv.shape[1] // NB_B
    ri = [_dot(cb[:, n * bs:(n + 1) * bs], wax_ref[n]) for n in range(NB_B)]
    r = _sigmoid(jnp.concatenate([p[:, 0:bs] for p in ri], axis=-1) + ba)
    i = _sigmoid(jnp.concatenate([p[:, bs:2 * bs] for p in ri], axis=-1) + bx)
    log_a = (-RG_C) * r * _softplus(-lam)
    a = jnp.exp(log_a)
    u = jnp.sqrt(1.0 - jnp.exp(2.0 * log_a)) * i * conv
    return a, u


def _head_out(o_heads, gain, gate):
    return jnp.concatenate([_rms(o, gain) for o in o_heads], axis=-1) * _silu(gate)


def _rot_gates(zt, d):
    lane = lax.broadcasted_iota(jnp.int32, (zt.shape[0], LANES), 1)
    out = []
    for i in range(3):
        first = jnp.where(lane < GATE_RANK, zt[:, (i + 1) * d:(i + 1) * d + LANES],
                          zt[:, i * d:i * d + LANES])
        out.append(jnp.concatenate([first, zt[:, i * d + LANES:(i + 1) * d]], axis=1))
    return out


def _merge(x, gate, gates, y_a, y_b, y_c, wbr_ref, wout_ref):
    merged = (_sigmoid(gates[0]) * _dot(y_a.astype(BF16), wbr_ref[0])
              + _sigmoid(gates[1]) * _dot(y_b.astype(BF16), wbr_ref[1])
              + _sigmoid(gates[2]) * _dot(y_c.astype(BF16), wbr_ref[2]))
    return x + gate * _dot(merged.astype(BF16), wout_ref[...])


PUMP_RATE = 1.0


def _no_pump(count=PUMP_RATE):
    del count


def _split3(x):
    hi = x.astype(BF16)
    r1 = x - hi.astype(F32)
    mid = r1.astype(BF16)
    lo = (r1 - mid.astype(F32)).astype(BF16)
    return hi, mid, lo


def _chunk_cumsum(g, chunk, tri):
    n = g.shape[0]
    hi, mid, lo = _split3(g)
    out = []
    for c in range(n // chunk):
        rs = slice(c * chunk, (c + 1) * chunk)
        out.append(_dot(tri, lo[rs]) + _dot(tri, mid[rs]) + _dot(tri, hi[rs]))
    return jnp.concatenate(out, axis=0) if len(out) > 1 else out[0]


def _level_operands(q, k, g, chunk, tri, pump=_no_pump):
    n, w = g.shape
    g = g * LOG2E
    rin = lax.broadcasted_iota(jnp.int32, (n, w), 0) & (chunk - 1)
    b = _chunk_cumsum(g, chunk, tri)
    xs = []
    n_levels = chunk.bit_length() - 1
    for l in range(n_levels):
        m = 1 << l
        right = (rin & m) != 0
        if l == 0:
            dlt = jnp.where(right, g, 0.0)
        elif l == 1:
            gm1 = pltpu.roll(g, 1, 0)
            gp1 = pltpu.roll(g, n - 1, 0)
            r4 = rin & 3
            dlt = jnp.where(r4 == 0, gp1, jnp.where(r4 == 1, 0.0, jnp.where(r4 == 2, g, g + gm1)))
        else:
            p = 2 * m
            b3 = b.reshape(n // p, p, w)
            ref = jnp.broadcast_to(b3[:, m - 1:m, :], (n // p, p, w)).reshape(n, w)
            dlt = -jnp.abs(b - ref)
        xs.append((jnp.where(right, q, k) * jnp.exp2(dlt)).astype(BF16))
        pump()
    b3 = b.reshape(n // chunk, chunk, w)
    bend3 = b3[:, chunk - 1:chunk, :]
    bend = jnp.broadcast_to(bend3, (n // chunk, chunk, w)).reshape(n, w)
    qe = (q * jnp.exp2(b)).astype(BF16)
    kd = (k * jnp.exp2(bend - b)).astype(BF16)
    ebend = jnp.exp2(bend3)
    return qe, kd, ebend, xs


def _level_masks(chunk):
    ri = lax.broadcasted_iota(jnp.int32, (chunk, chunk), 0)
    ci = lax.broadcasted_iota(jnp.int32, (chunk, chunk), 1)
    masks = []
    for l in range(chunk.bit_length() - 1):
        m = 1 << l
        same_parent = (ri >> (l + 1)) == (ci >> (l + 1))
        masks.append(same_parent & ((ri & m) != 0) & ((ci & m) == 0))
    tri = jnp.where(ri >= ci, 1.0, 0.0).astype(BF16)
    return masks, ri == ci, tri


def _recurrence_tile(q, k, v, g, st_ref, keep, n_heads, chunk, masks, eye, tri, pump=_no_pump):
    n = q.shape[0]
    dk = q.shape[1] // n_heads
    dv = v.shape[1] // n_heads
    qe, kd, ebend, xs = _level_operands(q, k, g, chunk, tri, pump)
    vb = v.astype(BF16)
    qk = q * k
    outs = []
    for h in range(n_heads):
        ks = slice(h * dk, (h + 1) * dk)
        vs = slice(h * dv, (h + 1) * dv)
        st = st_ref[h] * keep
        o_chunks = []
        for c in range(n // chunk):
            rs = slice(c * chunk, (c + 1) * chunk)
            dq = jnp.sum(qk[rs, ks], axis=-1, keepdims=True)
            scores = jnp.where(eye, dq, 0.0)
            for l, x in enumerate(xs):
                xl = x[rs, ks]
                scores = jnp.where(masks[l], _dot_nt(xl, xl), scores)
            vc = vb[rs, vs]
            o = _dot_nt(qe[rs, ks], st.astype(BF16)) + _dot(scores.astype(BF16), vc)
            o_chunks.append(o)
            st = st * ebend[c][:, ks] + _dot_tn(vc, kd[rs, ks])
        st_ref[h] = st
        outs.append(jnp.concatenate(o_chunks, axis=0) if len(o_chunks) > 1 else o_chunks[0])
    return outs


def _compose_scan(a, u, idx, axis, length, pump):
    s = 1
    while s < length:
        valid = idx >= s
        a_sh = pltpu.roll(a, s, axis)
        u_sh = pltpu.roll(u, s, axis)
        u = jnp.where(valid, a * u_sh + u, u)
        a = jnp.where(valid, a * a_sh, a)
        pump()
        s *= 2
    return a, u


def _linear_scan(a, u, h0, pump=_no_pump):
    n = a.shape[0]
    a_cum, u_cum = _compose_scan(a, u, lax.broadcasted_iota(jnp.int32, a.shape, 0), 0, n, pump)
    h = u_cum + a_cum * h0
    return h, h[n - 1:n]


def _mixer_kernel(xc_ref, xp_ref, sh_ref, sc_ref, gt_ref, gain_ref, win_ref, wtl_ref, lbp_ref, hn_ref,
                  cw_ref, rgp_ref, rwax_ref, wal_ref, bal_ref, gn_ref,
                  wbr_ref, wout_ref,
                  o_ref, sa_ref, cv_ref, hr_ref, sg_ref,
                  z0, z1, sta, stc, cvs, hrs, *, dims, tpb):
    d_a, d_b, dk_tot, d_c = dims
    s = pl.program_id(0)
    n, d = xc_ref.shape
    o_b = 4 * d_a
    o_c = o_b + 2 * d_b
    o_t = o_c + 2 * dk_tot + 2 * d_c
    o_end = o_t + 3 * d
    dk_c = dk_tot // H_C
    prev = s - 1

    @pl.when(s == 0)
    def _():
        z1[...] = jnp.zeros_like(z1)
        sta[...] = jnp.zeros_like(sta)
        stc[...] = jnp.zeros_like(stc)
        cvs[...] = jnp.zeros_like(cvs)
        hrs[...] = jnp.zeros_like(hrs)

    def step(zw, zr):
        h = _ada_norm(xc_ref[...], gain_ref[...], sh_ref[0], sc_ref[0]).astype(BF16)
        piece = 2 * LANES
        todo = list(range(0, o_end, piece)) + [o_end]

        credit = [0.0]

        def pump(count=PUMP_RATE):
            credit[0] += count
            while credit[0] >= 1.0:
                credit[0] -= 1.0
                if todo:
                    c0 = todo.pop(0)
                    if c0 < o_end:
                        zw[:, c0:c0 + piece] = _dot(h, win_ref[:, c0:c0 + piece])
                    else:
                        zw[:, c0:c0 + LANES] = _dot(h, wtl_ref[...])

        keep = jnp.where(lax.rem(prev, tpb) == 0, 0.0, 1.0).astype(F32)
        masks, eye, tri = _level_masks(REC_CHUNK)
        x = xp_ref[...]

        lbp = lbp_ref[...]
        pump(2.0)
        q, k, log_f = _hgrn_gates(zr[:, 0:d_a], zr[:, d_a:2 * d_a], lbp[0:1], lbp[1:2], lbp[2:3])
        pump(2.0)
        oa = _recurrence_tile(q, k, zr[:, 2 * d_a:3 * d_a], log_f, sta, keep,
                              H_A, REC_CHUNK, masks, eye, tri, pump)
        y_a = _head_out(oa, hn_ref[...], zr[:, 3 * d_a:4 * d_a])
        pump(2.0)

        bx = zr[:, o_b:o_b + d_b]
        cw = cw_ref[...]
        rgp = rgp_ref[...]
        ext = jnp.concatenate([cvs[...] * keep, bx], axis=0)
        conv = rgp[0:1] + cw[3:4] * bx
        for jj in range(CONV_W - 1):
            off = SUBLANES - (CONV_W - 1) + jj
            conv = conv + cw[jj:jj + 1] * ext[off:off + n]
        cvs[...] = bx[n - SUBLANES:n]
        pump(2.0)
        a, u = _rglru_gates(conv, rwax_ref, rgp[1:2], rgp[2:3], rgp[3:4])
        pump(2.0)
        hseq, hlast = _linear_scan(a, u, hrs[0:1] * keep, pump)
        hrs[...] = jnp.broadcast_to(hlast, hrs.shape)
        y_b = _gelu_tanh(zr[:, o_b + d_b:o_c]) * hseq

        log_fc = _gla_log_f(zr[:, o_t:o_t + LANES], wal_ref, bal_ref[...])
        oc = _recurrence_tile(zr[:, o_c:o_c + dk_tot] * (dk_c ** -0.5),
                              zr[:, o_c + dk_tot:o_c + 2 * dk_tot],
                              zr[:, o_c + 2 * dk_tot:o_c + 2 * dk_tot + d_c], log_fc, stc, keep,
                              H_C, REC_CHUNK, masks, eye, tri, pump)
        y_c = _head_out(oc, gn_ref[...], zr[:, o_c + 2 * dk_tot + d_c:o_t])
        pump(len(todo))

        gates = _rot_gates(zr[:, o_t:o_end + LANES], d)
        o_ref[...] = _merge(x, gt_ref[0], gates, y_a, y_b, y_c, wbr_ref, wout_ref)

    @pl.when(lax.rem(s, 2) == 0)
    def _():
        step(z0, z1)

    @pl.when(lax.rem(s, 2) == 1)
    def _():
        step(z1, z0)

    @pl.when(jnp.logical_and(s >= 1, lax.rem(prev, tpb) == tpb - 1))
    def _():
        for hh in range(H_A):
            sa_ref[hh] = sta[hh].T
        for hp in range(H_C // 2):
            pair = jnp.concatenate([stc[2 * hp], stc[2 * hp + 1]], axis=1).T
            sg_ref[2 * hp] = pair[0:dk_c]
            sg_ref[2 * hp + 1] = pair[dk_c:2 * dk_c]
        cv_ref[...] = cvs[SUBLANES - (CONV_W - 1):SUBLANES]
        hr_ref[...] = hrs[0:1]


def _mixer_call(x, mod4, gain, win, lbp, hnorm, cw, rgp, rwax, wal, bal, gnorm, wbr, wout,
                layer, batch, seq, dims):
    n, d = x.shape
    d_a, d_b, dk_tot, d_c = dims
    tile = min(MIX_TILE, seq)
    assert seq % tile == 0 and tile % REC_CHUNK == 0
    tpb = seq // tile
    dk_a, dv_a = d_a // H_A, d_a // H_A
    dk_c, dv_c = dk_tot // H_C, d_c // H_C
    nmain = 4 * d_a + 2 * d_b + 2 * dk_tot + 2 * d_c + 3 * d
    nin = nmain + LANES
    assert win.shape[2] == nin and nmain % (2 * LANES) == 0
    nt = batch * tpb
    cur = lambda s: jnp.minimum(s, nt - 1)
    prv = lambda s: jnp.maximum(s - 1, 0)
    mspec = lambda col, t: pl.BlockSpec((None, 1, 1, d), lambda s: (layer, t(s) // tpb, 0, col))
    lsel = lambda *rest: (lambda s: (layer,) + rest)
    in_specs = [
        pl.BlockSpec((tile, d), lambda s: (cur(s), 0)),
        pl.BlockSpec((tile, d), lambda s: (prv(s), 0)),
        mspec(3, cur), mspec(4, cur), mspec(5, prv),
        pl.BlockSpec((None, None, 1, d), lsel(1, 0, 0)),
        pl.BlockSpec((None, d, nmain), lsel(0, 0), pipeline_mode=pl.Buffered(1)),
        pl.BlockSpec((None, d, LANES), lsel(0, nmain // LANES)),
        pl.BlockSpec((None, 3, d_a), lsel(0, 0)),
        pl.BlockSpec((None, 1, dv_a), lsel(0, 0)),
        pl.BlockSpec((None, CONV_W, d_b), lsel(0, 0)),
        pl.BlockSpec((None, 4, d_b), lsel(0, 0)),
        pl.BlockSpec((None, NB_B, d_b // NB_B, 2 * d_b // NB_B), lsel(0, 0, 0)),
        pl.BlockSpec((None, LANES, dk_tot), lsel(0, 0)),
        pl.BlockSpec((None, 1, dk_tot), lsel(0, 0)),
        pl.BlockSpec((None, 1, dv_c), lsel(0, 0)),
        pl.BlockSpec((None, 3, d_a, d), lsel(0, 0, 0), pipeline_mode=pl.Buffered(1)),
        pl.BlockSpec((None, d, d), lsel(0, 0), pipeline_mode=pl.Buffered(1)),
    ]
    out_specs = [
        pl.BlockSpec((tile, d), lambda s: (prv(s), 0)),
        pl.BlockSpec((None, H_A, dk_a, dv_a), lambda s: (prv(s) // tpb, 0, 0, 0)),
        pl.BlockSpec((None, CONV_W - 1, d_b), lambda s: (prv(s) // tpb, 0, 0)),
        pl.BlockSpec((None, 1, d_b), lambda s: (prv(s) // tpb, 0, 0)),
        pl.BlockSpec((None, H_C, dk_c, dv_c), lambda s: (prv(s) // tpb, 0, 0, 0)),
    ]
    out_shape = [
        jax.ShapeDtypeStruct((n, d), F32),
        jax.ShapeDtypeStruct((batch, H_A, dk_a, dv_a), F32),
        jax.ShapeDtypeStruct((batch, CONV_W - 1, d_b), F32),
        jax.ShapeDtypeStruct((batch, 1, d_b), F32),
        jax.ShapeDtypeStruct((batch, H_C, dk_c, dv_c), F32),
    ]
    scratch = [
        pltpu.VMEM((tile, nin), F32),
        pltpu.VMEM((tile, nin), F32),
        pltpu.VMEM((H_A, dv_a, dk_a), F32),
        pltpu.VMEM((H_C, dv_c, dk_c), F32),
        pltpu.VMEM((SUBLANES, d_b), F32),
        pltpu.VMEM((SUBLANES, d_b), F32),
    ]
    return pl.pallas_call(
        functools.partial(_mixer_kernel, dims=dims, tpb=tpb),
        grid=(nt + 1,),
        in_specs=in_specs,
        out_specs=out_specs,
        out_shape=out_shape,
        scratch_shapes=scratch,
        compiler_params=pltpu.CompilerParams(
            dimension_semantics=("arbitrary",), vmem_limit_bytes=VMEM_LIMIT_BYTES),
        name="mixer_prompt",
    )(x, x, mod4, mod4, mod4, gain, win, win, lbp, hnorm, cw, rgp, rwax, wal, bal,
      gnorm, wbr, wout)


def _dec_pre_kernel(x_ref, sh_ref, sc_ref, gain_ref, win_ref, wtl_ref, lbp_ref, cw_ref, rgp_ref,
                    rwax_ref, wal_ref, bal_ref, cvin_ref, hrin_ref,
                    sv_ref, post_ref, cvout_ref, hrout_ref, *, dims):
    d_a, d_b, dk_tot, d_c = dims
    x = x_ref[...]
    d = x.shape[1]
    h = _ada_norm(x, gain_ref[...], sh_ref[0], sc_ref[0]).astype(BF16)
    o_b = 4 * d_a
    o_c = o_b + 2 * d_b
    o_t = o_c + 2 * dk_tot + 2 * d_c
    o_end = o_t + 3 * d

    za = _dot(h, win_ref[:, 0:o_b])
    lbp = lbp_ref[...]
    q, k, log_f = _hgrn_gates(za[:, 0:d_a], za[:, d_a:2 * d_a], lbp[0:1], lbp[1:2], lbp[2:3])

    zb = _dot(h, win_ref[:, o_b:o_c])
    bx = zb[:, 0:d_b]
    cw = cw_ref[...]
    rgp = rgp_ref[...]
    cvin = cvin_ref[...]
    conv = rgp[0:1] + cw[3:4] * bx
    for jj in range(CONV_W - 1):
        conv = conv + cw[jj:jj + 1] * cvin[:, jj * d_b:(jj + 1) * d_b]
    a, u = _rglru_gates(conv, rwax_ref, rgp[1:2], rgp[2:3], rgp[3:4])
    hnew = u + a * hrin_ref[...]
    y_b = _gelu_tanh(zb[:, d_b:2 * d_b]) * hnew
    cvout_ref[...] = jnp.concatenate([cvin[:, d_b:(CONV_W - 1) * d_b], bx], axis=-1)
    hrout_ref[...] = hnew

    zc = _dot(h, win_ref[:, o_c:o_t])
    zt = jnp.concatenate([_dot(h, win_ref[:, o_t:o_end]), _dot(h, wtl_ref[...])], axis=-1)
    log_fc = _gla_log_f(zt[:, 0:LANES], wal_ref, bal_ref[...])
    dk_c = dk_tot // H_C

    sv_ref[...] = jnp.concatenate(
        [q, jnp.exp(log_f), k, za[:, 2 * d_a:3 * d_a],
         zc[:, 0:dk_tot] * (dk_c ** -0.5), jnp.exp(log_fc), zc[:, dk_tot:2 * dk_tot],
         zc[:, 2 * dk_tot:2 * dk_tot + d_c]], axis=-1)
    post_ref[...] = jnp.concatenate(
        [za[:, 3 * d_a:4 * d_a], y_b, zc[:, 2 * dk_tot + d_c:2 * dk_tot + 2 * d_c]]
        + _rot_gates(zt, d), axis=-1)


def _dec_pre_call(x, mod4, gain, win, lbp, cw, rgp, rwax, wal, bal, cv_all, hr_all, layer, dims):
    n, d = x.shape
    d_a, d_b, dk_tot, d_c = dims
    nmain = win.shape[2] - LANES
    sv_w = 4 * d_a + 3 * dk_tot + d_c
    post_w = d_a + d_b + d_c + 3 * d
    mspec = lambda col: pl.BlockSpec((None, 1, n, d), lambda i: (layer, 0, 0, col))
    lsel = lambda *rest: (lambda i: (layer,) + rest)
    in_specs = [
        pl.BlockSpec((n, d), lambda i: (0, 0)),
        mspec(3), mspec(4),
        pl.BlockSpec((None, None, 1, d), lsel(1, 0, 0)),
        pl.BlockSpec((None, d, nmain), lsel(0, 0), pipeline_mode=pl.Buffered(1)),
        pl.BlockSpec((None, d, LANES), lsel(0, nmain // LANES)),
        pl.BlockSpec((None, 3, d_a), lsel(0, 0)),
        pl.BlockSpec((None, CONV_W, d_b), lsel(0, 0)),
        pl.BlockSpec((None, 4, d_b), lsel(0, 0)),
        pl.BlockSpec((None, NB_B, d_b // NB_B, 2 * d_b // NB_B), lsel(0, 0, 0)),
        pl.BlockSpec((None, LANES, dk_tot), lsel(0, 0)),
        pl.BlockSpec((None, 1, dk_tot), lsel(0, 0)),
        pl.BlockSpec((None, n, (CONV_W - 1) * d_b), lsel(0, 0)),
        pl.BlockSpec((None, n, d_b), lsel(0, 0)),
    ]
    full = lambda w: pl.BlockSpec((n, w), lambda i: (0, 0))
    return pl.pallas_call(
        functools.partial(_dec_pre_kernel, dims=dims),
        grid=(1,),
        in_specs=in_specs,
        out_specs=[full(sv_w), full(post_w), full((CONV_W - 1) * d_b), full(d_b)],
        out_shape=[jax.ShapeDtypeStruct((n, sv_w), F32), jax.ShapeDtypeStruct((n, post_w), F32),
                   jax.ShapeDtypeStruct((n, (CONV_W - 1) * d_b), F32),
                   jax.ShapeDtypeStruct((n, d_b), F32)],
        compiler_params=pltpu.CompilerParams(
            dimension_semantics=("arbitrary",), vmem_limit_bytes=VMEM_LIMIT_BYTES),
        name="mixer_sample_pre",
    )(x, mod4, mod4, gain, win, win, lbp, cw, rgp, rwax, wal, bal, cv_all, hr_all)


def _columns(rows):
    pad = jnp.zeros((LANES - rows.shape[0], LANES), F32)
    return jnp.concatenate([rows, pad], axis=0).T


def _dec_state_kernel(sv_ref, sa_ref, sg_ref, *rest, dims):
    oa_ref, oc_ref, sa_out, sg_out = rest[-4:]
    d_a, d_b, dk_tot, d_c = dims
    sv = sv_ref[...]
    g = sv.shape[0]
    dk_a = d_a // H_A
    dk_c = dk_tot // H_C
    dv_c = d_c // H_C
    o_q, o_f, o_k, o_v = 0, d_a, 2 * d_a, 3 * d_a
    for hh in range(H_A):
        qt = _columns(sv[:, o_q + hh * dk_a:o_q + (hh + 1) * dk_a])
        ft = _columns(sv[:, o_f + hh * dk_a:o_f + (hh + 1) * dk_a])
        kt = _columns(sv[:, o_k + hh * dk_a:o_k + (hh + 1) * dk_a])
        for jj in range(g):
            v = sv[jj:jj + 1, o_v + hh * dk_a:o_v + (hh + 1) * dk_a]
            s_new = ft[:, jj:jj + 1] * sa_ref[jj, hh] + kt[:, jj:jj + 1] * v
            sa_out[jj, hh] = s_new
            oa_ref[jj:jj + 1, hh * dk_a:(hh + 1) * dk_a] = jnp.sum(
                qt[:, jj:jj + 1] * s_new, axis=0, keepdims=True)
    c_q = 4 * d_a
    c_f = c_q + dk_tot
    c_k = c_f + dk_tot
    c_v = c_k + dk_tot
    per_tile = LANES // dk_c
    for tp in range(dk_tot // LANES):
        qt = _columns(sv[:, c_q + tp * LANES:c_q + (tp + 1) * LANES])
        ft = _columns(sv[:, c_f + tp * LANES:c_f + (tp + 1) * LANES])
        kt = _columns(sv[:, c_k + tp * LANES:c_k + (tp + 1) * LANES])
        for hp in range(per_tile):
            hh = tp * per_tile + hp
            rs = slice(hp * dk_c, (hp + 1) * dk_c)
            for jj in range(g):
                v = sv[jj:jj + 1, c_v + hh * dv_c:c_v + (hh + 1) * dv_c]
                s_new = ft[rs, jj:jj + 1] * sg_ref[jj, hh] + kt[rs, jj:jj + 1] * v
                sg_out[jj, hh] = s_new
                oc_ref[jj:jj + 1, hh * dv_c:(hh + 1) * dv_c] = jnp.sum(
                    qt[rs, jj:jj + 1] * s_new, axis=0, keepdims=True)


def _dec_state_call(sv, st_a, st_c, layer, dims, stacked):
    n, sv_w = sv.shape
    d_a, d_b, dk_tot, d_c = dims
    depth = st_a.shape[0]
    g = DEC_GROUP
    assert n % g == 0
    dk_a = d_a // H_A
    dk_c, dv_c = dk_tot // H_C, d_c // H_C
    in_specs = [
        pl.BlockSpec((g, sv_w), lambda i: (i, 0)),
        pl.BlockSpec((None, g, H_A, dk_a, dk_a), lambda i: (layer, i, 0, 0, 0)),
        pl.BlockSpec((None, g, H_C, dk_c, dv_c), lambda i: (layer, i, 0, 0, 0)),
    ]
    in_specs += [pl.BlockSpec(memory_space=pl.ANY), pl.BlockSpec(memory_space=pl.ANY)]
    args = [sv, st_a, st_c, *stacked]
    aliases = {3: 2, 4: 3}
    return pl.pallas_call(
        functools.partial(_dec_state_kernel, dims=dims),
        grid=(n // g,),
        in_specs=in_specs,
        out_specs=[
            pl.BlockSpec((g, d_a), lambda i: (i, 0)),
            pl.BlockSpec((g, d_c), lambda i: (i, 0)),
            pl.BlockSpec((None, g, H_A, dk_a, dk_a), lambda i: (layer, i, 0, 0, 0)),
            pl.BlockSpec((None, g, H_C, dk_c, dv_c), lambda i: (layer, i, 0, 0, 0)),
        ],
        out_shape=[
            jax.ShapeDtypeStruct((n, d_a), F32),
            jax.ShapeDtypeStruct((n, d_c), F32),
            jax.ShapeDtypeStruct((depth, n, H_A, dk_a, dk_a), F32),
            jax.ShapeDtypeStruct((depth, n, H_C, dk_c, dv_c), F32),
        ],
        input_output_aliases=aliases,
        compiler_params=pltpu.CompilerParams(
            dimension_semantics=("arbitrary",), vmem_limit_bytes=VMEM_LIMIT_BYTES),
        name="mixer_sample_state",
    )(*args)


def _dec_post_kernel(x_ref, gt_ref, post_ref, oa_ref, oc_ref, hn_ref, gn_ref, wbr_ref, wout_ref,
                     o_ref, *, dims):
    d_a, d_b, dk_tot, d_c = dims
    x = x_ref[...]
    post = post_ref[...]
    oa = oa_ref[...]
    oc = oc_ref[...]
    dv_a = d_a // H_A
    dv_c = d_c // H_C
    y_a = _head_out([oa[:, hh * dv_a:(hh + 1) * dv_a] for hh in range(H_A)], hn_ref[...],
                    post[:, 0:d_a])
    y_b = post[:, d_a:d_a + d_b]
    y_c = _head_out([oc[:, hh * dv_c:(hh + 1) * dv_c] for hh in range(H_C)], gn_ref[...],
                    post[:, d_a + d_b:d_a + d_b + d_c])
    off = d_a + d_b + d_c
    d = x.shape[1]
    gates = [post[:, off + i * d:off + (i + 1) * d] for i in range(3)]
    o_ref[...] = _merge(x, gt_ref[0], gates, y_a, y_b, y_c, wbr_ref, wout_ref)


def _dec_post_call(x, mod4, post, oa, oc, hnorm, gnorm, wbr, wout, layer, dims):
    n, d = x.shape
    d_a, d_b, dk_tot, d_c = dims
    lsel = lambda *rest: (lambda i: (layer,) + rest)
    full = lambda w: pl.BlockSpec((n, w), lambda i: (0, 0))
    return pl.pallas_call(
        functools.partial(_dec_post_kernel, dims=dims),
        grid=(1,),
        in_specs=[
            full(d),
            pl.BlockSpec((None, 1, n, d), lambda i: (layer, 0, 0, 5)),
            full(post.shape[1]), full(d_a), full(d_c),
            pl.BlockSpec((None, 1, d_a // H_A), lsel(0, 0)),
            pl.BlockSpec((None, 1, d_c // H_C), lsel(0, 0)),
            pl.BlockSpec((None, 3, d_a, d), lsel(0, 0, 0), pipeline_mode=pl.Buffered(1)),
            pl.BlockSpec((None, d, d), lsel(0, 0), pipeline_mode=pl.Buffered(1)),
        ],
        out_specs=full(d),
        out_shape=jax.ShapeDtypeStruct((n, d), F32),
        compiler_params=pltpu.CompilerParams(
            dimension_semantics=("arbitrary",), vmem_limit_bytes=VMEM_LIMIT_BYTES),
        name="mixer_sample_post",
    )(x, mod4, post, oa, oc, hnorm, gnorm, wbr, wout)


def kernel(x_prompt, x_sample, c_prompt, c_sample, state_hgrn, state_conv, state_rglru, state_gla,
           w_ada, b_ada, norm_gain, w_ffn_in, w_ffn_out, w_in, hgrn_lb_logits, hgrn_norm,
           conv_w, conv_b, rg_wa, rg_ba, rg_wx, rg_bx, rg_lambda, gla_w_alpha, gla_b_alpha,
           gla_norm, w_br_a, w_br_b, w_br_c, w_out, final_norm):
    batch, seq, d = x_prompt.shape
    nd = x_sample.shape[0]
    depth = w_in.shape[0]
    d_a = hgrn_lb_logits.shape[1]
    d_b = conv_b.shape[1]
    dk_tot = gla_b_alpha.shape[1]
    d_c = w_br_c.shape[1]
    dims = (d_a, d_b, dk_tot, d_c)
    nm = w_ada.shape[2]

    wi = w_ffn_in.astype(BF16)
    wo = w_ffn_out.astype(BF16)
    o_ca = 4 * d_a + 2 * d_b + 2 * dk_tot + 2 * d_c
    assert w_in.shape[2] == o_ca + 3 * d + GATE_RANK
    win = jnp.pad(w_in, ((0, 0), (0, 0), (0, LANES - GATE_RANK))).astype(BF16)
    wal = jnp.concatenate(
        [gla_w_alpha, jnp.zeros((depth, LANES - GATE_RANK, dk_tot), gla_w_alpha.dtype)],
        axis=1).astype(BF16)
    bal = gla_b_alpha.reshape(depth, 1, dk_tot)
    wbr = jnp.roll(jnp.stack([w_br_a, w_br_b, w_br_c], axis=1), GATE_RANK, axis=-1).astype(BF16)
    wout = jnp.roll(w_out, GATE_RANK, axis=1).astype(BF16)
    rwax = jnp.concatenate([rg_wa, rg_wx], axis=-1).astype(BF16)
    rgp = jnp.stack([conv_b, rg_ba, rg_bx, rg_lambda], axis=1)
    hnorm = hgrn_norm.reshape(depth, 1, -1)
    gnorm = gla_norm.reshape(depth, 1, -1)
    lb = jnp.cumsum(jax.nn.softmax(hgrn_lb_logits.astype(F32), axis=0), axis=0)
    lb = lb - lb[0]
    lbp = jnp.stack([jnp.log(lb), jnp.log1p(-lb), 1.0 - lb], axis=1)

    norm_gain = norm_gain.reshape(depth, N_SUB, 1, d)
    c_all = jnp.concatenate([c_prompt, c_sample], axis=0)
    mod = _ada_call(c_all, w_ada, b_ada)
    mod_p = mod[:, :batch].reshape(depth, batch, 1, nm)
    mod_s = mod[:, batch:].reshape(depth, 1, nd, nm)

    ffn_tile = min(FFN_TILE, seq)
    tpg = seq // ffn_tile
    x = x_prompt.reshape(batch * seq, d)
    sa_p, cv_p, hr_p, sg_p = [], [], [], []
    for l in range(depth):
        x = _ffn_call(x, mod_p, norm_gain, wi, wo, l, 0, 0, ffn_tile, tpg)
        x, sa, cv, hr, sg = _mixer_call(x, mod_p, norm_gain, win, lbp, hnorm, conv_w, rgp, rwax,
                                        wal, bal, gnorm, wbr, wout, l, batch, seq, dims)
        x = _ffn_call(x, mod_p, norm_gain, wi, wo, l, 1, 2, ffn_tile, tpg,
                      final_gain=final_norm if l == depth - 1 else None)
        sa_p.append(sa); cv_p.append(cv); hr_p.append(hr[:, 0]); sg_p.append(sg)
    y_prompt = x.reshape(batch, seq, d)

    xs = x_sample.reshape(nd, d)
    cv_all = state_conv.reshape(depth, nd, (CONV_W - 1) * d_b)
    cv_s, hr_s = [], []
    stacked = (jnp.zeros(state_hgrn.shape, F32), jnp.zeros(state_gla.shape, F32))
    for l in range(depth):
        xs = _ffn_call(xs, mod_s, norm_gain, wi, wo, l, 0, 0, nd, 1)
        sv, post, cv, hr = _dec_pre_call(xs, mod_s, norm_gain, win, lbp, conv_w, rgp, rwax,
                                         wal, bal, cv_all, state_rglru, l, dims)
        oa, oc, sa_s, sg_s = _dec_state_call(sv, state_hgrn, state_gla, l, dims, stacked)
        stacked = (sa_s, sg_s)
        xs = _dec_post_call(xs, mod_s, post, oa, oc, hnorm, gnorm, wbr, wout, l, dims)
        xs = _ffn_call(xs, mod_s, norm_gain, wi, wo, l, 1, 2, nd, 1,
                       final_gain=final_norm if l == depth - 1 else None)
        cv_s.append(cv.reshape(nd, CONV_W - 1, d_b)); hr_s.append(hr)
    y_sample = xs.reshape(nd, 1, d)

    st = jnp.stack
    return (y_prompt, y_sample, st(sa_p), sa_s, st(cv_p), st(cv_s), st(hr_p), st(hr_s),
            st(sg_p), sg_s)
```

```python
import functools
import math

import jax
import jax.numpy as jnp
from jax import lax
from jax.experimental import pallas as pl
from jax.experimental.pallas import tpu as pltpu

F32 = jnp.float32
BF16 = jnp.bfloat16

EPS = 1e-6
H_A = 4
NB_B = 4
CONV_W = 4
RG_C = 8.0
H_C = 4
GATE_RANK = 16
GLA_TAU = 16.0
N_SUB = 3
LOG2E = 1.4426950408889634

LANES = 128
SUBLANES = 8
VMEM_LIMIT_BYTES = 56 * 1024 * 1024

REC_CHUNK = 128
MIX_TILE = 256
FFN_TILE = 512
DEC_GROUP = 8


def _dot(a, b):
    return jnp.dot(a, b, preferred_element_type=F32)


def _dot_nt(a, b):
    return lax.dot_general(a, b, (((1,), (1,)), ((), ())), preferred_element_type=F32)


def _dot_tn(a, b):
    return lax.dot_general(a, b, (((0,), (0,)), ((), ())), preferred_element_type=F32)


def _sigmoid(x):
    return 0.5 * jnp.tanh(0.5 * x) + 0.5


def _silu(x):
    h = 0.5 * x
    return h * jnp.tanh(h) + h


def _softplus(x):
    return jnp.maximum(x, 0.0) + jnp.log(1.0 + jnp.exp(-jnp.abs(x)))


def _log_sigmoid(x):
    return -_softplus(-x)


def _gelu_tanh(x):
    c = math.sqrt(2.0 / math.pi)
    return x * (0.5 * (1.0 + jnp.tanh(c * (x + 0.044715 * (x * x * x)))))


def _rms(x, gain):
    ms = jnp.mean(x * x, axis=-1, keepdims=True)
    return x * lax.rsqrt(ms + EPS) * gain


def _ada_norm(x, gain, shift, scale):
    return _rms(x, gain) * (1.0 + scale) + shift


def _ada_kernel(c_ref, w_ref, b_ref, o_ref):
    o_ref[...] = _dot(c_ref[...].astype(BF16), w_ref[...].astype(BF16)) + b_ref[...]


def _ada_call(c_all, w_ada, b_ada):
    depth, d, nm = w_ada.shape
    rows = c_all.shape[0]
    tn = 1536
    assert nm % tn == 0
    return pl.pallas_call(
        _ada_kernel,
        grid=(depth, nm // tn),
        in_specs=[
            pl.BlockSpec((rows, d), lambda l, n: (0, 0)),
            pl.BlockSpec((None, d, tn), lambda l, n: (l, 0, n)),
            pl.BlockSpec((None, 1, tn), lambda l, n: (l, 0, n)),
        ],
        out_specs=pl.BlockSpec((None, rows, tn), lambda l, n: (l, 0, n)),
        out_shape=jax.ShapeDtypeStruct((depth, rows, nm), F32),
        compiler_params=pltpu.CompilerParams(
            dimension_semantics=("arbitrary", "arbitrary"),
            vmem_limit_bytes=VMEM_LIMIT_BYTES),
        name="ada_mod",
    )(c_all, w_ada, b_ada.reshape(depth, 1, nm))


def _cast_pad_kernel(w_ref, o_ref):
    n_in = w_ref.shape[1]
    o_ref[:, 0:n_in] = w_ref[...].astype(BF16)
    o_ref[:, n_in:] = jnp.zeros((o_ref.shape[0], o_ref.shape[1] - n_in), BF16)


def _cast_pad_call(w, padded):
    depth, d, n_in = w.shape
    rows = LANES
    return pl.pallas_call(
        _cast_pad_kernel,
        grid=(depth, d // rows),
        in_specs=[pl.BlockSpec((None, rows, n_in), lambda l, r: (l, r, 0))],
        out_specs=pl.BlockSpec((None, rows, padded), lambda l, r: (l, r, 0)),
        out_shape=jax.ShapeDtypeStruct((depth, d, padded), BF16),
        compiler_params=pltpu.CompilerParams(
            dimension_semantics=("arbitrary", "arbitrary"), vmem_limit_bytes=VMEM_LIMIT_BYTES),
        name="cast_pad_w_in",
    )(w)


def _ffn_kernel(x_ref, sh_ref, sc_ref, gt_ref, gain_ref, wi_ref, wo_ref, *rest, n_chunks, final, fill):
    rest = list(rest)
    fg_ref = rest.pop(0) if final else None
    o_ref = rest.pop(0)
    if fill:
        zero_ref = rest.pop(0)
        zero_ref[...] = jnp.zeros_like(zero_ref)
    x = x_ref[...]
    h = _ada_norm(x, gain_ref[...], sh_ref[0], sc_ref[0]).astype(BF16)
    f = wo_ref.shape[0]
    fc = f // n_chunks
    acc = None
    for c in range(n_chunks):
        g = _dot(h, wi_ref[:, c * fc:(c + 1) * fc])
        u = _dot(h, wi_ref[:, f + c * fc:f + (c + 1) * fc])
        a = (_silu(g) * u).astype(BF16)
        part = _dot(a, wo_ref[c * fc:(c + 1) * fc, :])
        acc = part if acc is None else acc + part
    y = x + (0.5 * gt_ref[0]) * acc
    if final:
        y = _rms(y, fg_ref[...])
    o_ref[...] = y


def _mod_spec(rows_per_group, d, layer, col, group_of):
    return pl.BlockSpec((None, 1, rows_per_group, d), lambda i: (layer, group_of(i), 0, col))


def _ffn_call(x, mod4, gain, wi, wo, layer, which, sub, tile, tiles_per_group, final_gain=None,
              fill_rows=0):
    n, d = x.shape
    f = wo.shape[2]
    r = mod4.shape[2]
    group_of = (lambda i: i // tiles_per_group)
    final = final_gain is not None
    in_specs = [
        pl.BlockSpec((tile, d), lambda i: (i, 0)),
        _mod_spec(r, d, layer, 3 * sub + 0, group_of),
        _mod_spec(r, d, layer, 3 * sub + 1, group_of),
        _mod_spec(r, d, layer, 3 * sub + 2, group_of),
        pl.BlockSpec((None, None, 1, d), lambda i: (layer, sub, 0, 0)),
        pl.BlockSpec((None, None, d, 2 * f), lambda i: (layer, which, 0, 0),
                     pipeline_mode=pl.Buffered(1)),
        pl.BlockSpec((None, None, f, d), lambda i: (layer, which, 0, 0),
                     pipeline_mode=pl.Buffered(1)),
    ]
    args = [x, mod4, mod4, mod4, gain, wi, wo]
    if final:
        in_specs.append(pl.BlockSpec((1, d), lambda i: (0, 0)))
        args.append(final_gain.reshape(1, d))
    steps = n // tile
    out_specs = [pl.BlockSpec((tile, d), lambda i: (i, 0))]
    out_shape = [jax.ShapeDtypeStruct((n, d), F32)]
    if fill_rows:
        assert fill_rows % (steps * SUBLANES) == 0
        out_specs.append(pl.BlockSpec((fill_rows // steps, d), lambda i: (i, 0)))
        out_shape.append(jax.ShapeDtypeStruct((fill_rows, d), F32))
    out = pl.pallas_call(
        functools.partial(_ffn_kernel, n_chunks=1, final=final, fill=bool(fill_rows)),
        grid=(steps,),
        in_specs=in_specs,
        out_specs=out_specs,
        out_shape=out_shape,
        compiler_params=pltpu.CompilerParams(
            dimension_semantics=("arbitrary",), vmem_limit_bytes=VMEM_LIMIT_BYTES),
        name="ffn",
    )(*args)
    return tuple(out) if fill_rows else out[0]


def _hgrn_gates(a_q, a_f, loglb, log1mlb, omlb):
    q = _silu(a_q)
    c = log1mlb + _log_sigmoid(a_f)
    m = jnp.maximum(loglb, c)
    log_f = m + jnp.log(1.0 + jnp.exp(-jnp.abs(loglb - c)))
    k = omlb * _sigmoid(-a_f)
    return q, k, log_f


def _gla_log_f(c_a, wal_ref, bal):
    return _log_sigmoid(_dot(c_a.astype(BF16), wal_ref[...]) + bal) * (1.0 / GLA_TAU)


def _rglru_gates(conv, wax_ref, ba, bx, lam):
    cb = conv.astype(BF16)
    bs = conv.shape[1] // NB_B
    ri = [_dot(cb[:, n * bs:(n + 1) * bs], wax_ref[n]) for n in range(NB_B)]
    r = _sigmoid(jnp.concatenate([p[:, 0:bs] for p in ri], axis=-1) + ba)
    i = _sigmoid(jnp.concatenate([p[:, bs:2 * bs] for p in ri], axis=-1) + bx)
    log_a = (-RG_C) * r * _softplus(-lam)
    a = jnp.exp(log_a)
    u = jnp.sqrt(1.0 - jnp.exp(2.0 * log_a)) * i * conv
    return a, u


def _head_out(o_heads, gain, gate):
    return jnp.concatenate([_rms(o, gain) for o in o_heads], axis=-1) * _silu(gate)


def _rot_gates(zt, d):
    lane = lax.broadcasted_iota(jnp.int32, (zt.shape[0], LANES), 1)
    out = []
    for i in range(3):
        first = jnp.where(lane < GATE_RANK, zt[:, (i + 1) * d:(i + 1) * d + LANES],
                          zt[:, i * d:i * d + LANES])
        out.append(jnp.concatenate([first, zt[:, i * d + LANES:(i + 1) * d]], axis=1))
    return out


def _merge(x, gate, gates, y_a, y_b, y_c, wbr_ref, wout_ref):
    merged = (_sigmoid(gates[0]) * _dot(y_a.astype(BF16), wbr_ref[0])
              + _sigmoid(gates[1]) * _dot(y_b.astype(BF16), wbr_ref[1])
              + _sigmoid(gates[2]) * _dot(y_c.astype(BF16), wbr_ref[2]))
    return x + gate * _dot(merged.astype(BF16), wout_ref[...])


PUMP_RATE = 1.0


def _no_pump(count=PUMP_RATE):
    del count


def _split3(x):
    hi = x.astype(BF16)
    r1 = x - hi.astype(F32)
    mid = r1.astype(BF16)
    lo = (r1 - mid.astype(F32)).astype(BF16)
    return hi, mid, lo


def _chunk_cumsum(g, chunk, tri):
    n = g.shape[0]
    hi, mid, lo = _split3(g)
    out = []
    for c in range(n // chunk):
        rs = slice(c * chunk, (c + 1) * chunk)
        out.append(_dot(tri, lo[rs]) + _dot(tri, mid[rs]) + _dot(tri, hi[rs]))
    return jnp.concatenate(out, axis=0) if len(out) > 1 else out[0]


def _level_operands(q, k, g, chunk, tri, pump=_no_pump):
    n, w = g.shape
    g = g * LOG2E
    rin = lax.broadcasted_iota(jnp.int32, (n, w), 0) & (chunk - 1)
    b = _chunk_cumsum(g, chunk, tri)
    xs = []
    n_levels = chunk.bit_length() - 1
    for l in range(n_levels):
        m = 1 << l
        right = (rin & m) != 0
        if l == 0:
            dlt = jnp.where(right, g, 0.0)
        elif l == 1:
            gm1 = pltpu.roll(g, 1, 0)
            gp1 = pltpu.roll(g, n - 1, 0)
            r4 = rin & 3
            dlt = jnp.where(r4 == 0, gp1, jnp.where(r4 == 1, 0.0, jnp.where(r4 == 2, g, g + gm1)))
        else:
            p = 2 * m
            b3 = b.reshape(n // p, p, w)
            ref = jnp.broadcast_to(b3[:, m - 1:m, :], (n // p, p, w)).reshape(n, w)
            dlt = -jnp.abs(b - ref)
        xs.append((jnp.where(right, q, k) * jnp.exp2(dlt)).astype(BF16))
        pump()
    b3 = b.reshape(n // chunk, chunk, w)
    bend3 = b3[:, chunk - 1:chunk, :]
    bend = jnp.broadcast_to(bend3, (n // chunk, chunk, w)).reshape(n, w)
    qe = (q * jnp.exp2(b)).astype(BF16)
    kd = (k * jnp.exp2(bend - b)).astype(BF16)
    ebend = jnp.exp2(bend3)
    return qe, kd, ebend, xs


def _level_masks(chunk):
    ri = lax.broadcasted_iota(jnp.int32, (chunk, chunk), 0)
    ci = lax.broadcasted_iota(jnp.int32, (chunk, chunk), 1)
    masks = []
    for l in range(chunk.bit_length() - 1):
        m = 1 << l
        same_parent = (ri >> (l + 1)) == (ci >> (l + 1))
        masks.append(same_parent & ((ri & m) != 0) & ((ci & m) == 0))
    tri = jnp.where(ri >= ci, 1.0, 0.0).astype(BF16)
    return masks, ri == ci, tri


def _recurrence_tile(q, k, v, g, st_ref, keep, n_heads, chunk, masks, eye, tri, pump=_no_pump):
    n = q.shape[0]
    dk = q.shape[1] // n_heads
    dv = v.shape[1] // n_heads
    qe, kd, ebend, xs = _level_operands(q, k, g, chunk, tri, pump)
    vb = v.astype(BF16)
    qk = q * k
    outs = []
    for h in range(n_heads):
        ks = slice(h * dk, (h + 1) * dk)
        vs = slice(h * dv, (h + 1) * dv)
        st = st_ref[h] * keep
        o_chunks = []
        for c in range(n // chunk):
            rs = slice(c * chunk, (c + 1) * chunk)
            dq = jnp.sum(qk[rs, ks], axis=-1, keepdims=True)
            scores = jnp.where(eye, dq, 0.0)
            for l, x in enumerate(xs):
                xl = x[rs, ks]
                scores = jnp.where(masks[l], _dot_nt(xl, xl), scores)
            vc = vb[rs, vs]
            o = _dot_nt(qe[rs, ks], st.astype(BF16)) + _dot(scores.astype(BF16), vc)
            o_chunks.append(o)
            st = st * ebend[c][:, ks] + _dot_tn(vc, kd[rs, ks])
        st_ref[h] = st
        outs.append(jnp.concatenate(o_chunks, axis=0) if len(o_chunks) > 1 else o_chunks[0])
    return outs


def _compose_scan(a, u, idx, axis, length, pump):
    s = 1
    while s < length:
        valid = idx >= s
        a_sh = pltpu.roll(a, s, axis)
        u_sh = pltpu.roll(u, s, axis)
        u = jnp.where(valid, a * u_sh + u, u)
        a = jnp.where(valid, a * a_sh, a)
        pump()
        s *= 2
    return a, u


def _linear_scan(a, u, h0, pump=_no_pump):
    n = a.shape[0]
    a_cum, u_cum = _compose_scan(a, u, lax.broadcasted_iota(jnp.int32, a.shape, 0), 0, n, pump)
    h = u_cum + a_cum * h0
    return h, h[n - 1:n]


def _mixer_kernel(xc_ref, xp_ref, sh_ref, sc_ref, gt_ref, gain_ref, win_ref, wtl_ref, lbp_ref, hn_ref,
                  cw_ref, rgp_ref, rwax_ref, wal_ref, bal_ref, gn_ref,
                  wbr_ref, wout_ref,
                  o_ref, sa_ref, cv_ref, hr_ref, sg_ref,
                  z0, z1, sta, stc, cvs, hrs, *, dims, tpb):
    d_a, d_b, dk_tot, d_c = dims
    s = pl.program_id(0)
    n, d = xc_ref.shape
    o_b = 4 * d_a
    o_c = o_b + 2 * d_b
    o_t = o_c + 2 * dk_tot + 2 * d_c
    o_end = o_t + 3 * d
    dk_c = dk_tot // H_C
    prev = s - 1

    @pl.when(s == 0)
    def _():
        z1[...] = jnp.zeros_like(z1)
        sta[...] = jnp.zeros_like(sta)
        stc[...] = jnp.zeros_like(stc)
        cvs[...] = jnp.zeros_like(cvs)
        hrs[...] = jnp.zeros_like(hrs)

    def step(zw, zr):
        h = _ada_norm(xc_ref[...], gain_ref[...], sh_ref[0], sc_ref[0]).astype(BF16)
        piece = 2 * LANES
        todo = list(range(0, o_end, piece)) + [o_end]

        credit = [0.0]

        def pump(count=PUMP_RATE):
            credit[0] += count
            while credit[0] >= 1.0:
                credit[0] -= 1.0
                if todo:
                    c0 = todo.pop(0)
                    if c0 < o_end:
                        zw[:, c0:c0 + piece] = _dot(h, win_ref[:, c0:c0 + piece])
                    else:
                        zw[:, c0:c0 + LANES] = _dot(h, wtl_ref[...])

        keep = jnp.where(lax.rem(prev, tpb) == 0, 0.0, 1.0).astype(F32)
        masks, eye, tri = _level_masks(REC_CHUNK)
        x = xp_ref[...]

        lbp = lbp_ref[...]
        pump(2.0)
        q, k, log_f = _hgrn_gates(zr[:, 0:d_a], zr[:, d_a:2 * d_a], lbp[0:1], lbp[1:2], lbp[2:3])
        pump(2.0)
        oa = _recurrence_tile(q, k, zr[:, 2 * d_a:3 * d_a], log_f, sta, keep,
                              H_A, REC_CHUNK, masks, eye, tri, pump)
        y_a = _head_out(oa, hn_ref[...], zr[:, 3 * d_a:4 * d_a])
        pump(2.0)

        bx = zr[:, o_b:o_b + d_b]
        cw = cw_ref[...]
        rgp = rgp_ref[...]
        ext = jnp.concatenate([cvs[...] * keep, bx], axis=0)
        conv = rgp[0:1] + cw[3:4] * bx
        for jj in range(CONV_W - 1):
            off = SUBLANES - (CONV_W - 1) + jj
            conv = conv + cw[jj:jj + 1] * ext[off:off + n]
        cvs[...] = bx[n - SUBLANES:n]
        pump(2.0)
        a, u = _rglru_gates(conv, rwax_ref, rgp[1:2], rgp[2:3], rgp[3:4])
        pump(2.0)
        hseq, hlast = _linear_scan(a, u, hrs[0:1] * keep, pump)
        hrs[...] = jnp.broadcast_to(hlast, hrs.shape)
        y_b = _gelu_tanh(zr[:, o_b + d_b:o_c]) * hseq

        log_fc = _gla_log_f(zr[:, o_t:o_t + LANES], wal_ref, bal_ref[...])
        oc = _recurrence_tile(zr[:, o_c:o_c + dk_tot] * (dk_c ** -0.5),
                              zr[:, o_c + dk_tot:o_c + 2 * dk_tot],
                              zr[:, o_c + 2 * dk_tot:o_c + 2 * dk_tot + d_c], log_fc, stc, keep,
                              H_C, REC_CHUNK, masks, eye, tri, pump)
        y_c = _head_out(oc, gn_ref[...], zr[:, o_c + 2 * dk_tot + d_c:o_t])
        pump(len(todo))

        gates = _rot_gates(zr[:, o_t:o_end + LANES], d)
        o_ref[...] = _merge(x, gt_ref[0], gates, y_a, y_b, y_c, wbr_ref, wout_ref)

    @pl.when(lax.rem(s, 2) == 0)
    def _():
        step(z0, z1)

    @pl.when(lax.rem(s, 2) == 1)
    def _():
        step(z1, z0)

    @pl.when(jnp.logical_and(s >= 1, lax.rem(prev, tpb) == tpb - 1))
    def _():
        for hh in range(H_A):
            sa_ref[hh] = sta[hh].T
        for hp in range(H_C // 2):
            pair = jnp.concatenate([stc[2 * hp], stc[2 * hp + 1]], axis=1).T
            sg_ref[2 * hp] = pair[0:dk_c]
            sg_ref[2 * hp + 1] = pair[dk_c:2 * dk_c]
        cv_ref[...] = cvs[SUBLANES - (CONV_W - 1):SUBLANES]
        hr_ref[...] = hrs[0:1]


def _mixer_call(x, mod4, gain, win, lbp, hnorm, cw, rgp, rwax, wal, bal, gnorm, wbr, wout,
                layer, batch, seq, dims):
    n, d = x.shape
    d_a, d_b, dk_tot, d_c = dims
    tile = min(MIX_TILE, seq)
    assert seq % tile == 0 and tile % REC_CHUNK == 0
    tpb = seq // tile
    dk_a, dv_a = d_a // H_A, d_a // H_A
    dk_c, dv_c = dk_tot // H_C, d_c // H_C
    nmain = 4 * d_a + 2 * d_b + 2 * dk_tot + 2 * d_c + 3 * d
    nin = nmain + LANES
    assert win.shape[2] == nin and nmain % (2 * LANES) == 0
    nt = batch * tpb
    cur = lambda s: jnp.minimum(s, nt - 1)
    prv = lambda s: jnp.maximum(s - 1, 0)
    mspec = lambda col, t: pl.BlockSpec((None, 1, 1, d), lambda s: (layer, t(s) // tpb, 0, col))
    lsel = lambda *rest: (lambda s: (layer,) + rest)
    in_specs = [
        pl.BlockSpec((tile, d), lambda s: (cur(s), 0)),
        pl.BlockSpec((tile, d), lambda s: (prv(s), 0)),
        mspec(3, cur), mspec(4, cur), mspec(5, prv),
        pl.BlockSpec((None, None, 1, d), lsel(1, 0, 0)),
        pl.BlockSpec((None, d, nmain), lsel(0, 0), pipeline_mode=pl.Buffered(1)),
        pl.BlockSpec((None, d, LANES), lsel(0, nmain // LANES)),
        pl.BlockSpec((None, 3, d_a), lsel(0, 0)),
        pl.BlockSpec((None, 1, dv_a), lsel(0, 0)),
        pl.BlockSpec((None, CONV_W, d_b), lsel(0, 0)),
        pl.BlockSpec((None, 4, d_b), lsel(0, 0)),
        pl.BlockSpec((None, NB_B, d_b // NB_B, 2 * d_b // NB_B), lsel(0, 0, 0)),
        pl.BlockSpec((None, LANES, dk_tot), lsel(0, 0)),
        pl.BlockSpec((None, 1, dk_tot), lsel(0, 0)),
        pl.BlockSpec((None, 1, dv_c), lsel(0, 0)),
        pl.BlockSpec((None, 3, d_a, d), lsel(0, 0, 0), pipeline_mode=pl.Buffered(1)),
        pl.BlockSpec((None, d, d), lsel(0, 0), pipeline_mode=pl.Buffered(1)),
    ]
    out_specs = [
        pl.BlockSpec((tile, d), lambda s: (prv(s), 0)),
        pl.BlockSpec((None, H_A, dk_a, dv_a), lambda s: (prv(s) // tpb, 0, 0, 0)),
        pl.BlockSpec((None, CONV_W - 1, d_b), lambda s: (prv(s) // tpb, 0, 0)),
        pl.BlockSpec((None, 1, d_b), lambda s: (prv(s) // tpb, 0, 0)),
        pl.BlockSpec((None, H_C, dk_c, dv_c), lambda s: (prv(s) // tpb, 0, 0, 0)),
    ]
    out_shape = [
        jax.ShapeDtypeStruct((n, d), F32),
        jax.ShapeDtypeStruct((batch, H_A, dk_a, dv_a), F32),
        jax.ShapeDtypeStruct((batch, CONV_W - 1, d_b), F32),
        jax.ShapeDtypeStruct((batch, 1, d_b), F32),
        jax.ShapeDtypeStruct((batch, H_C, dk_c, dv_c), F32),
    ]
    scratch = [
        pltpu.VMEM((tile, nin), F32),
        pltpu.VMEM((tile, nin), F32),
        pltpu.VMEM((H_A, dv_a, dk_a), F32),
        pltpu.VMEM((H_C, dv_c, dk_c), F32),
        pltpu.VMEM((SUBLANES, d_b), F32),
        pltpu.VMEM((SUBLANES, d_b), F32),
    ]
    return pl.pallas_call(
        functools.partial(_mixer_kernel, dims=dims, tpb=tpb),
        grid=(nt + 1,),
        in_specs=in_specs,
        out_specs=out_specs,
        out_shape=out_shape,
        scratch_shapes=scratch,
        compiler_params=pltpu.CompilerParams(
            dimension_semantics=("arbitrary",), vmem_limit_bytes=VMEM_LIMIT_BYTES),
        name="mixer_prompt",
    )(x, x, mod4, mod4, mod4, gain, win, win, lbp, hnorm, cw, rgp, rwax, wal, bal,
      gnorm, wbr, wout)


def _dec_pre_kernel(x_ref, sh_ref, sc_ref, gain_ref, win_ref, wtl_ref, lbp_ref, cw_ref, rgp_ref,
                    rwax_ref, wal_ref, bal_ref, cvin_ref, hrin_ref,
                    sv_ref, post_ref, cvout_ref, hrout_ref, *, dims):
    d_a, d_b, dk_tot, d_c = dims
    x = x_ref[...]
    d = x.shape[1]
    h = _ada_norm(x, gain_ref[...], sh_ref[0], sc_ref[0]).astype(BF16)
    o_b = 4 * d_a
    o_c = o_b + 2 * d_b
    o_t = o_c + 2 * dk_tot + 2 * d_c
    o_end = o_t + 3 * d

    za = _dot(h, win_ref[:, 0:o_b])
    lbp = lbp_ref[...]
    q, k, log_f = _hgrn_gates(za[:, 0:d_a], za[:, d_a:2 * d_a], lbp[0:1], lbp[1:2], lbp[2:3])

    zb = _dot(h, win_ref[:, o_b:o_c])
    bx = zb[:, 0:d_b]
    cw = cw_ref[...]
    rgp = rgp_ref[...]
    cvin = cvin_ref[...]
    conv = rgp[0:1] + cw[3:4] * bx
    for jj in range(CONV_W - 1):
        conv = conv + cw[jj:jj + 1] * cvin[:, jj * d_b:(jj + 1) * d_b]
    a, u = _rglru_gates(conv, rwax_ref, rgp[1:2], rgp[2:3], rgp[3:4])
    hnew = u + a * hrin_ref[...]
    y_b = _gelu_tanh(zb[:, d_b:2 * d_b]) * hnew
    cvout_ref[...] = jnp.concatenate([cvin[:, d_b:(CONV_W - 1) * d_b], bx], axis=-1)
    hrout_ref[...] = hnew

    zc = _dot(h, win_ref[:, o_c:o_t])
    zt = jnp.concatenate([_dot(h, win_ref[:, o_t:o_end]), _dot(h, wtl_ref[...])], axis=-1)
    log_fc = _gla_log_f(zt[:, 0:LANES], wal_ref, bal_ref[...])
    dk_c = dk_tot // H_C

    sv_ref[...] = jnp.concatenate(
        [q, jnp.exp(log_f), k, za[:, 2 * d_a:3 * d_a],
         zc[:, 0:dk_tot] * (dk_c ** -0.5), jnp.exp(log_fc), zc[:, dk_tot:2 * dk_tot],
         zc[:, 2 * dk_tot:2 * dk_tot + d_c]], axis=-1)
    post_ref[...] = jnp.concatenate(
        [za[:, 3 * d_a:4 * d_a], y_b, zc[:, 2 * dk_tot + d_c:2 * dk_tot + 2 * d_c]]
        + _rot_gates(zt, d), axis=-1)


def _dec_pre_call(x, mod4, gain, win, lbp, cw, rgp, rwax, wal, bal, cv_all, hr_all, layer, dims):
    n, d = x.shape
    d_a, d_b, dk_tot, d_c = dims
    nmain = win.shape[2] - LANES
    sv_w = 4 * d_a + 3 * dk_tot + d_c
    post_w = d_a + d_b + d_c + 3 * d
    mspec = lambda col: pl.BlockSpec((None, 1, n, d), lambda i: (layer, 0, 0, col))
    lsel = lambda *rest: (lambda i: (layer,) + rest)
    in_specs = [
        pl.BlockSpec((n, d), lambda i: (0, 0)),
        mspec(3), mspec(4),
        pl.BlockSpec((None, None, 1, d), lsel(1, 0, 0)),
        pl.BlockSpec((None, d, nmain), lsel(0, 0), pipeline_mode=pl.Buffered(1)),
        pl.BlockSpec((None, d, LANES), lsel(0, nmain // LANES)),
        pl.BlockSpec((None, 3, d_a), lsel(0, 0)),
        pl.BlockSpec((None, CONV_W, d_b), lsel(0, 0)),
        pl.BlockSpec((None, 4, d_b), lsel(0, 0)),
        pl.BlockSpec((None, NB_B, d_b // NB_B, 2 * d_b // NB_B), lsel(0, 0, 0)),
        pl.BlockSpec((None, LANES, dk_tot), lsel(0, 0)),
        pl.BlockSpec((None, 1, dk_tot), lsel(0, 0)),
        pl.BlockSpec((None, n, (CONV_W - 1) * d_b), lsel(0, 0)),
        pl.BlockSpec((None, n, d_b), lsel(0, 0)),
    ]
    full = lambda w: pl.BlockSpec((n, w), lambda i: (0, 0))
    return pl.pallas_call(
        functools.partial(_dec_pre_kernel, dims=dims),
        grid=(1,),
        in_specs=in_specs,
        out_specs=[full(sv_w), full(post_w), full((CONV_W - 1) * d_b), full(d_b)],
        out_shape=[jax.ShapeDtypeStruct((n, sv_w), F32), jax.ShapeDtypeStruct((n, post_w), F32),
                   jax.ShapeDtypeStruct((n, (CONV_W - 1) * d_b), F32),
                   jax.ShapeDtypeStruct((n, d_b), F32)],
        compiler_params=pltpu.CompilerParams(
            dimension_semantics=("arbitrary",), vmem_limit_bytes=VMEM_LIMIT_BYTES),
        name="mixer_sample_pre",
    )(x, mod4, mod4, gain, win, win, lbp, cw, rgp, rwax, wal, bal, cv_all, hr_all)


def _columns(rows):
    pad = jnp.zeros((LANES - rows.shape[0], LANES), F32)
    return jnp.concatenate([rows, pad], axis=0).T


def _dec_state_kernel(sv_ref, sa_ref, sg_ref, *rest, dims):
    oa_ref, oc_ref, sa_out, sg_out = rest[-4:]
    d_a, d_b, dk_tot, d_c = dims
    sv = sv_ref[...]
    g = sv.shape[0]
    dk_a = d_a // H_A
    dk_c = dk_tot // H_C
    dv_c = d_c // H_C
    o_q, o_f, o_k, o_v = 0, d_a, 2 * d_a, 3 * d_a
    for hh in range(H_A):
        qt = _columns(sv[:, o_q + hh * dk_a:o_q + (hh + 1) * dk_a])
        ft = _columns(sv[:, o_f + hh * dk_a:o_f + (hh + 1) * dk_a])
        kt = _columns(sv[:, o_k + hh * dk_a:o_k + (hh + 1) * dk_a])
        for jj in range(g):
            v = sv[jj:jj + 1, o_v + hh * dk_a:o_v + (hh + 1) * dk_a]
            s_new = ft[:, jj:jj + 1] * sa_ref[jj, hh] + kt[:, jj:jj + 1] * v
            sa_out[jj, hh] = s_new
            oa_ref[jj:jj + 1, hh * dk_a:(hh + 1) * dk_a] = jnp.sum(
                qt[:, jj:jj + 1] * s_new, axis=0, keepdims=True)
    c_q = 4 * d_a
    c_f = c_q + dk_tot
    c_k = c_f + dk_tot
    c_v = c_k + dk_tot
    per_tile = LANES // dk_c
    for tp in range(dk_tot // LANES):
        qt = _columns(sv[:, c_q + tp * LANES:c_q + (tp + 1) * LANES])
        ft = _columns(sv[:, c_f + tp * LANES:c_f + (tp + 1) * LANES])
        kt = _columns(sv[:, c_k + tp * LANES:c_k + (tp + 1) * LANES])
        for hp in range(per_tile):
            hh = tp * per_tile + hp
            rs = slice(hp * dk_c, (hp + 1) * dk_c)
            for jj in range(g):
                v = sv[jj:jj + 1, c_v + hh * dv_c:c_v + (hh + 1) * dv_c]
                s_new = ft[rs, jj:jj + 1] * sg_ref[jj, hh] + kt[rs, jj:jj + 1] * v
                sg_out[jj, hh] = s_new
                oc_ref[jj:jj + 1, hh * dv_c:(hh + 1) * dv_c] = jnp.sum(
                    qt[rs, jj:jj + 1] * s_new, axis=0, keepdims=True)


def _dec_state_call(sv, st_a, st_c, layer, dims, stacked):
    n, sv_w = sv.shape
    d_a, d_b, dk_tot, d_c = dims
    depth = st_a.shape[0]
    g = DEC_GROUP
    assert n % g == 0
    dk_a = d_a // H_A
    dk_c, dv_c = dk_tot // H_C, d_c // H_C
    in_specs = [
        pl.BlockSpec((g, sv_w), lambda i: (i, 0)),
        pl.BlockSpec((None, g, H_A, dk_a, dk_a), lambda i: (layer, i, 0, 0, 0)),
        pl.BlockSpec((None, g, H_C, dk_c, dv_c), lambda i: (layer, i, 0, 0, 0)),
    ]
    in_specs += [pl.BlockSpec(memory_space=pl.ANY), pl.BlockSpec(memory_space=pl.ANY)]
    args = [sv, st_a, st_c, *stacked]
    aliases = {3: 2, 4: 3}
    return pl.pallas_call(
        functools.partial(_dec_state_kernel, dims=dims),
        grid=(n // g,),
        in_specs=in_specs,
        out_specs=[
            pl.BlockSpec((g, d_a), lambda i: (i, 0)),
            pl.BlockSpec((g, d_c), lambda i: (i, 0)),
            pl.BlockSpec((None, g, H_A, dk_a, dk_a), lambda i: (layer, i, 0, 0, 0)),
            pl.BlockSpec((None, g, H_C, dk_c, dv_c), lambda i: (layer, i, 0, 0, 0)),
        ],
        out_shape=[
            jax.ShapeDtypeStruct((n, d_a), F32),
            jax.ShapeDtypeStruct((n, d_c), F32),
            jax.ShapeDtypeStruct((depth, n, H_A, dk_a, dk_a), F32),
            jax.ShapeDtypeStruct((depth, n, H_C, dk_c, dv_c), F32),
        ],
        input_output_aliases=aliases,
        compiler_params=pltpu.CompilerParams(
            dimension_semantics=("arbitrary",), vmem_limit_bytes=VMEM_LIMIT_BYTES),
        name="mixer_sample_state",
    )(*args)


def _dec_post_kernel(x_ref, gt_ref, post_ref, oa_ref, oc_ref, hn_ref, gn_ref, wbr_ref, wout_ref,
                     o_ref, *, dims):
    d_a, d_b, dk_tot, d_c = dims
    x = x_ref[...]
    post = post_ref[...]
    oa = oa_ref[...]
    oc = oc_ref[...]
    dv_a = d_a // H_A
    dv_c = d_c // H_C
    y_a = _head_out([oa[:, hh * dv_a:(hh + 1) * dv_a] for hh in range(H_A)], hn_ref[...],
                    post[:, 0:d_a])
    y_b = post[:, d_a:d_a + d_b]
    y_c = _head_out([oc[:, hh * dv_c:(hh + 1) * dv_c] for hh in range(H_C)], gn_ref[...],
                    post[:, d_a + d_b:d_a + d_b + d_c])
    off = d_a + d_b + d_c
    d = x.shape[1]
    gates = [post[:, off + i * d:off + (i + 1) * d] for i in range(3)]
    o_ref[...] = _merge(x, gt_ref[0], gates, y_a, y_b, y_c, wbr_ref, wout_ref)


def _dec_post_call(x, mod4, post, oa, oc, hnorm, gnorm, wbr, wout, layer, dims):
    n, d = x.shape
    d_a, d_b, dk_tot, d_c = dims
    lsel = lambda *rest: (lambda i: (layer,) + rest)
    full = lambda w: pl.BlockSpec((n, w), lambda i: (0, 0))
    return pl.pallas_call(
        functools.partial(_dec_post_kernel, dims=dims),
        grid=(1,),
        in_specs=[
            full(d),
            pl.BlockSpec((None, 1, n, d), lambda i: (layer, 0, 0, 5)),
            full(post.shape[1]), full(d_a), full(d_c),
            pl.BlockSpec((None, 1, d_a // H_A), lsel(0, 0)),
            pl.BlockSpec((None, 1, d_c // H_C), lsel(0, 0)),
            pl.BlockSpec((None, 3, d_a, d), lsel(0, 0, 0), pipeline_mode=pl.Buffered(1)),
            pl.BlockSpec((None, d, d), lsel(0, 0), pipeline_mode=pl.Buffered(1)),
        ],
        out_specs=full(d),
        out_shape=jax.ShapeDtypeStruct((n, d), F32),
        compiler_params=pltpu.CompilerParams(
            dimension_semantics=("arbitrary",), vmem_limit_bytes=VMEM_LIMIT_BYTES),
        name="mixer_sample_post",
    )(x, mod4, post, oa, oc, hnorm, gnorm, wbr, wout)


def kernel(x_prompt, x_sample, c_prompt, c_sample, state_hgrn, state_conv, state_rglru, state_gla,
           w_ada, b_ada, norm_gain, w_ffn_in, w_ffn_out, w_in, hgrn_lb_logits, hgrn_norm,
           conv_w, conv_b, rg_wa, rg_ba, rg_wx, rg_bx, rg_lambda, gla_w_alpha, gla_b_alpha,
           gla_norm, w_br_a, w_br_b, w_br_c, w_out, final_norm):
    batch, seq, d = x_prompt.shape
    nd = x_sample.shape[0]
    depth = w_in.shape[0]
    d_a = hgrn_lb_logits.shape[1]
    d_b = conv_b.shape[1]
    dk_tot = gla_b_alpha.shape[1]
    d_c = w_br_c.shape[1]
    dims = (d_a, d_b, dk_tot, d_c)
    nm = w_ada.shape[2]

    wi = w_ffn_in.astype(BF16)
    wo = w_ffn_out.astype(BF16)
    o_ca = 4 * d_a + 2 * d_b + 2 * dk_tot + 2 * d_c
    assert w_in.shape[2] == o_ca + 3 * d + GATE_RANK
    win = _cast_pad_call(w_in, w_in.shape[2] + LANES - GATE_RANK)
    wal = jnp.concatenate(
        [gla_w_alpha, jnp.zeros((depth, LANES - GATE_RANK, dk_tot), gla_w_alpha.dtype)],
        axis=1).astype(BF16)
    bal = gla_b_alpha.reshape(depth, 1, dk_tot)
    wbr = jnp.roll(jnp.stack([w_br_a, w_br_b, w_br_c], axis=1), GATE_RANK, axis=-1).astype(BF16)
    wout = jnp.roll(w_out, GATE_RANK, axis=1).astype(BF16)
    rwax = jnp.concatenate([rg_wa, rg_wx], axis=-1).astype(BF16)
    rgp = jnp.stack([conv_b, rg_ba, rg_bx, rg_lambda], axis=1)
    hnorm = hgrn_norm.reshape(depth, 1, -1)
    gnorm = gla_norm.reshape(depth, 1, -1)
    lb = jnp.cumsum(jax.nn.softmax(hgrn_lb_logits.astype(F32), axis=0), axis=0)
    lb = lb - lb[0]
    lbp = jnp.stack([jnp.log(lb), jnp.log1p(-lb), 1.0 - lb], axis=1)

    norm_gain = norm_gain.reshape(depth, N_SUB, 1, d)
    c_all = jnp.concatenate([c_prompt, c_sample], axis=0)
    mod = _ada_call(c_all, w_ada, b_ada)
    mod_p = mod[:, :batch].reshape(depth, batch, 1, nm)
    mod_s = mod[:, batch:].reshape(depth, 1, nd, nm)

    ffn_tile = min(FFN_TILE, seq)
    tpg = seq // ffn_tile
    x = x_prompt.reshape(batch * seq, d)
    sa_p, cv_p, hr_p, sg_p = [], [], [], []
    zero_a = zero_c = None
    for l in range(depth):
        x = _ffn_call(x, mod_p, norm_gain, wi, wo, l, 0, 0, ffn_tile, tpg,
                      fill_rows=state_hgrn.size // d if l == 0 else 0)
        if l == 0:
            x, zero_a = x
        x, sa, cv, hr, sg = _mixer_call(x, mod_p, norm_gain, win, lbp, hnorm, conv_w, rgp, rwax,
                                        wal, bal, gnorm, wbr, wout, l, batch, seq, dims)
        x = _ffn_call(x, mod_p, norm_gain, wi, wo, l, 1, 2, ffn_tile, tpg,
                      final_gain=final_norm if l == depth - 1 else None,
                      fill_rows=state_gla.size // d if l == 0 else 0)
        if l == 0:
            x, zero_c = x
        sa_p.append(sa); cv_p.append(cv); hr_p.append(hr[:, 0]); sg_p.append(sg)
    y_prompt = x.reshape(batch, seq, d)

    xs = x_sample.reshape(nd, d)
    cv_all = state_conv.reshape(depth, nd, (CONV_W - 1) * d_b)
    cv_s, hr_s = [], []
    stacked = (zero_a.reshape(state_hgrn.shape), zero_c.reshape(state_gla.shape))
    for l in range(depth):
        xs = _ffn_call(xs, mod_s, norm_gain, wi, wo, l, 0, 0, nd, 1)
        sv, post, cv, hr = _dec_pre_call(xs, mod_s, norm_gain, win, lbp, conv_w, rgp, rwax,
                                         wal, bal, cv_all, state_rglru, l, dims)
        oa, oc, sa_s, sg_s = _dec_state_call(sv, state_hgrn, state_gla, l, dims, stacked)
        stacked = (sa_s, sg_s)
        xs = _dec_post_call(xs, mod_s, post, oa, oc, hnorm, gnorm, wbr, wout, l, dims)
        xs = _ffn_call(xs, mod_s, norm_gain, wi, wo, l, 1, 2, nd, 1,
                       final_gain=final_norm if l == depth - 1 else None)
        cv_s.append(cv.reshape(nd, CONV_W - 1, d_b)); hr_s.append(hr)
    y_sample = xs.reshape(nd, 1, d)

    st = jnp.stack
    return (y_prompt, y_sample, st(sa_p), sa_s, st(cv_p), st(cv_s), st(hr_p), st(hr_s),
            st(sg_p), sg_s)
```

```python
import functools
import math

import jax
import jax.numpy as jnp
from jax import lax
from jax.experimental import pallas as pl
from jax.experimental.pallas import tpu as pltpu

F32 = jnp.float32
BF16 = jnp.bfloat16

EPS = 1e-6
H_A = 4
NB_B = 4
CONV_W = 4
RG_C = 8.0
H_C = 4
GATE_RANK = 16
GLA_TAU = 16.0
N_SUB = 3
LOG2E = 1.4426950408889634

LANES = 128
SUBLANES = 8
VMEM_LIMIT_BYTES = 56 * 1024 * 1024

REC_CHUNK = 128
MIX_TILE = 256
FFN_TILE = 512
DEC_GROUP = 8


def _dot(a, b):
    return jnp.dot(a, b, preferred_element_type=F32)


def _dot_nt(a, b):
    return lax.dot_general(a, b, (((1,), (1,)), ((), ())), preferred_element_type=F32)


def _dot_tn(a, b):
    return lax.dot_general(a, b, (((0,), (0,)), ((), ())), preferred_element_type=F32)


def _sigmoid(x):
    return 0.5 * jnp.tanh(0.5 * x) + 0.5


def _silu(x):
    h = 0.5 * x
    return h * jnp.tanh(h) + h


def _softplus(x):
    return jnp.maximum(x, 0.0) + jnp.log(1.0 + jnp.exp(-jnp.abs(x)))


def _log_sigmoid(x):
    return -_softplus(-x)


def _gelu_tanh(x):
    c = math.sqrt(2.0 / math.pi)
    return x * (0.5 * (1.0 + jnp.tanh(c * (x + 0.044715 * (x * x * x)))))


def _rms(x, gain):
    ms = jnp.mean(x * x, axis=-1, keepdims=True)
    return x * lax.rsqrt(ms + EPS) * gain


def _ada_norm(x, gain, shift, scale):
    return _rms(x, gain) * (1.0 + scale) + shift


def _ada_kernel(c_ref, w_ref, b_ref, o_ref):
    o_ref[...] = _dot(c_ref[...].astype(BF16), w_ref[...].astype(BF16)) + b_ref[...]


def _ada_call(c_all, w_ada, b_ada):
    depth, d, nm = w_ada.shape
    rows = c_all.shape[0]
    tn = 1536
    assert nm % tn == 0
    return pl.pallas_call(
        _ada_kernel,
        grid=(depth, nm // tn),
        in_specs=[
            pl.BlockSpec((rows, d), lambda l, n: (0, 0)),
            pl.BlockSpec((None, d, tn), lambda l, n: (l, 0, n)),
            pl.BlockSpec((None, 1, tn), lambda l, n: (l, 0, n)),
        ],
        out_specs=pl.BlockSpec((None, rows, tn), lambda l, n: (l, 0, n)),
        out_shape=jax.ShapeDtypeStruct((depth, rows, nm), F32),
        compiler_params=pltpu.CompilerParams(
            dimension_semantics=("arbitrary", "arbitrary"),
            vmem_limit_bytes=VMEM_LIMIT_BYTES),
        name="ada_mod",
    )(c_all, w_ada, b_ada.reshape(depth, 1, nm))


def _ffn_kernel(x_ref, sh_ref, sc_ref, gt_ref, gain_ref, wi_ref, wo_ref, *rest, n_chunks, final, fill):
    rest = list(rest)
    fg_ref = rest.pop(0) if final else None
    o_ref = rest.pop(0)
    if fill:
        zero_ref = rest.pop(0)
        zero_ref[...] = jnp.zeros_like(zero_ref)
    x = x_ref[...]
    h = _ada_norm(x, gain_ref[...], sh_ref[0], sc_ref[0]).astype(BF16)
    f = wo_ref.shape[0]
    fc = f // n_chunks
    acc = None
    for c in range(n_chunks):
        g = _dot(h, wi_ref[:, c * fc:(c + 1) * fc])
        u = _dot(h, wi_ref[:, f + c * fc:f + (c + 1) * fc])
        a = (_silu(g) * u).astype(BF16)
        part = _dot(a, wo_ref[c * fc:(c + 1) * fc, :])
        acc = part if acc is None else acc + part
    y = x + (0.5 * gt_ref[0]) * acc
    if final:
        y = _rms(y, fg_ref[...])
    o_ref[...] = y


def _mod_spec(rows_per_group, d, layer, col, group_of):
    return pl.BlockSpec((None, 1, rows_per_group, d), lambda i: (layer, group_of(i), 0, col))


def _ffn_call(x, mod4, gain, wi, wo, layer, which, sub, tile, tiles_per_group, final_gain=None,
              fill_shape=None):
    n, d = x.shape
    f = wo.shape[2]
    r = mod4.shape[2]
    group_of = (lambda i: i // tiles_per_group)
    final = final_gain is not None
    in_specs = [
        pl.BlockSpec((tile, d), lambda i: (i, 0)),
        _mod_spec(r, d, layer, 3 * sub + 0, group_of),
        _mod_spec(r, d, layer, 3 * sub + 1, group_of),
        _mod_spec(r, d, layer, 3 * sub + 2, group_of),
        pl.BlockSpec((None, None, 1, d), lambda i: (layer, sub, 0, 0)),
        pl.BlockSpec((None, None, d, 2 * f), lambda i: (layer, which, 0, 0),
                     pipeline_mode=pl.Buffered(1)),
        pl.BlockSpec((None, None, f, d), lambda i: (layer, which, 0, 0),
                     pipeline_mode=pl.Buffered(1)),
    ]
    args = [x, mod4, mod4, mod4, gain, wi, wo]
    if final:
        in_specs.append(pl.BlockSpec((1, d), lambda i: (0, 0)))
        args.append(final_gain.reshape(1, d))
    steps = n // tile
    out_specs = [pl.BlockSpec((tile, d), lambda i: (i, 0))]
    out_shape = [jax.ShapeDtypeStruct((n, d), F32)]
    if fill_shape is not None:
        lead, second = fill_shape[0], fill_shape[1]
        assert steps % lead == 0 and second % (steps // lead) == 0
        per_lead = steps // lead
        blk = second // per_lead
        out_specs.append(pl.BlockSpec((None, blk) + tuple(fill_shape[2:]),
                                      lambda i: (i // per_lead, i % per_lead, 0, 0, 0)))
        out_shape.append(jax.ShapeDtypeStruct(tuple(fill_shape), F32))
    out = pl.pallas_call(
        functools.partial(_ffn_kernel, n_chunks=1, final=final, fill=fill_shape is not None),
        grid=(steps,),
        in_specs=in_specs,
        out_specs=out_specs,
        out_shape=out_shape,
        compiler_params=pltpu.CompilerParams(
            dimension_semantics=("arbitrary",), vmem_limit_bytes=VMEM_LIMIT_BYTES),
        name="ffn",
    )(*args)
    return tuple(out) if fill_shape is not None else out[0]


def _hgrn_gates(a_q, a_f, loglb, log1mlb, omlb):
    q = _silu(a_q)
    c = log1mlb + _log_sigmoid(a_f)
    m = jnp.maximum(loglb, c)
    log_f = m + jnp.log(1.0 + jnp.exp(-jnp.abs(loglb - c)))
    k = omlb * _sigmoid(-a_f)
    return q, k, log_f


def _gla_log_f(c_a, wal_ref, bal):
    return _log_sigmoid(_dot(c_a.astype(BF16), wal_ref[...]) + bal) * (1.0 / GLA_TAU)


def _rglru_gates(conv, wax_ref, ba, bx, lam):
    cb = conv.astype(BF16)
    bs = conv.shape[1] // NB_B
    ri = [_dot(cb[:, n * bs:(n + 1) * bs], wax_ref[n]) for n in range(NB_B)]
    r = _sigmoid(jnp.concatenate([p[:, 0:bs] for p in ri], axis=-1) + ba)
    i = _sigmoid(jnp.concatenate([p[:, bs:2 * bs] for p in ri], axis=-1) + bx)
    log_a = (-RG_C) * r * _softplus(-lam)
    a = jnp.exp(log_a)
    u = jnp.sqrt(1.0 - jnp.exp(2.0 * log_a)) * i * conv
    return a, u


def _head_out(o_heads, gain, gate):
    return jnp.concatenate([_rms(o, gain) for o in o_heads], axis=-1) * _silu(gate)


def _rot_gates(zt, d):
    lane = lax.broadcasted_iota(jnp.int32, (zt.shape[0], LANES), 1)
    out = []
    for i in range(3):
        first = jnp.where(lane < GATE_RANK, zt[:, (i + 1) * d:(i + 1) * d + LANES],
                          zt[:, i * d:i * d + LANES])
        out.append(jnp.concatenate([first, zt[:, i * d + LANES:(i + 1) * d]], axis=1))
    return out


def _merge(x, gate, gates, y_a, y_b, y_c, wbr_ref, wout_ref):
    merged = (_sigmoid(gates[0]) * _dot(y_a.astype(BF16), wbr_ref[0])
              + _sigmoid(gates[1]) * _dot(y_b.astype(BF16), wbr_ref[1])
              + _sigmoid(gates[2]) * _dot(y_c.astype(BF16), wbr_ref[2]))
    return x + gate * _dot(merged.astype(BF16), wout_ref[...])


PUMP_RATE = 1.0


def _no_pump(count=PUMP_RATE):
    del count


def _split3(x):
    hi = x.astype(BF16)
    r1 = x - hi.astype(F32)
    mid = r1.astype(BF16)
    lo = (r1 - mid.astype(F32)).astype(BF16)
    return hi, mid, lo


def _chunk_cumsum(g, chunk, tri):
    n = g.shape[0]
    hi, mid, lo = _split3(g)
    out = []
    for c in range(n // chunk):
        rs = slice(c * chunk, (c + 1) * chunk)
        out.append(_dot(tri, lo[rs]) + _dot(tri, mid[rs]) + _dot(tri, hi[rs]))
    return jnp.concatenate(out, axis=0) if len(out) > 1 else out[0]


def _level_operands(q, k, g, chunk, tri, pump=_no_pump):
    n, w = g.shape
    g = g * LOG2E
    rin = lax.broadcasted_iota(jnp.int32, (n, w), 0) & (chunk - 1)
    b = _chunk_cumsum(g, chunk, tri)
    xs = []
    n_levels = chunk.bit_length() - 1
    for l in range(n_levels):
        m = 1 << l
        right = (rin & m) != 0
        if l == 0:
            dlt = jnp.where(right, g, 0.0)
        elif l == 1:
            gm1 = pltpu.roll(g, 1, 0)
            gp1 = pltpu.roll(g, n - 1, 0)
            r4 = rin & 3
            dlt = jnp.where(r4 == 0, gp1, jnp.where(r4 == 1, 0.0, jnp.where(r4 == 2, g, g + gm1)))
        else:
            p = 2 * m
            b3 = b.reshape(n // p, p, w)
            ref = jnp.broadcast_to(b3[:, m - 1:m, :], (n // p, p, w)).reshape(n, w)
            dlt = -jnp.abs(b - ref)
        xs.append((jnp.where(right, q, k) * jnp.exp2(dlt)).astype(BF16))
        pump()
    b3 = b.reshape(n // chunk, chunk, w)
    bend3 = b3[:, chunk - 1:chunk, :]
    bend = jnp.broadcast_to(bend3, (n // chunk, chunk, w)).reshape(n, w)
    qe = (q * jnp.exp2(b)).astype(BF16)
    kd = (k * jnp.exp2(bend - b)).astype(BF16)
    ebend = jnp.exp2(bend3)
    return qe, kd, ebend, xs


def _level_masks(chunk):
    ri = lax.broadcasted_iota(jnp.int32, (chunk, chunk), 0)
    ci = lax.broadcasted_iota(jnp.int32, (chunk, chunk), 1)
    masks = []
    for l in range(chunk.bit_length() - 1):
        m = 1 << l
        same_parent = (ri >> (l + 1)) == (ci >> (l + 1))
        masks.append(same_parent & ((ri & m) != 0) & ((ci & m) == 0))
    tri = jnp.where(ri >= ci, 1.0, 0.0).astype(BF16)
    return masks, ri == ci, tri


def _recurrence_tile(q, k, v, g, st_ref, keep, n_heads, chunk, masks, eye, tri, pump=_no_pump):
    n = q.shape[0]
    dk = q.shape[1] // n_heads
    dv = v.shape[1] // n_heads
    qe, kd, ebend, xs = _level_operands(q, k, g, chunk, tri, pump)
    vb = v.astype(BF16)
    qk = q * k
    outs = []
    for h in range(n_heads):
        ks = slice(h * dk, (h + 1) * dk)
        vs = slice(h * dv, (h + 1) * dv)
        st = st_ref[h] * keep
        o_chunks = []
        for c in range(n // chunk):
            rs = slice(c * chunk, (c + 1) * chunk)
            dq = jnp.sum(qk[rs, ks], axis=-1, keepdims=True)
            scores = jnp.where(eye, dq, 0.0)
            for l, x in enumerate(xs):
                xl = x[rs, ks]
                scores = jnp.where(masks[l], _dot_nt(xl, xl), scores)
            vc = vb[rs, vs]
            o = _dot_nt(qe[rs, ks], st.astype(BF16)) + _dot(scores.astype(BF16), vc)
            o_chunks.append(o)
            st = st * ebend[c][:, ks] + _dot_tn(vc, kd[rs, ks])
        st_ref[h] = st
        outs.append(jnp.concatenate(o_chunks, axis=0) if len(o_chunks) > 1 else o_chunks[0])
    return outs


def _compose_scan(a, u, idx, axis, length, pump):
    s = 1
    while s < length:
        valid = idx >= s
        a_sh = pltpu.roll(a, s, axis)
        u_sh = pltpu.roll(u, s, axis)
        u = jnp.where(valid, a * u_sh + u, u)
        a = jnp.where(valid, a * a_sh, a)
        pump()
        s *= 2
    return a, u


def _linear_scan(a, u, h0, pump=_no_pump):
    n = a.shape[0]
    a_cum, u_cum = _compose_scan(a, u, lax.broadcasted_iota(jnp.int32, a.shape, 0), 0, n, pump)
    h = u_cum + a_cum * h0
    return h, h[n - 1:n]


def _mixer_kernel(xc_ref, xp_ref, sh_ref, sc_ref, gt_ref, gain_ref, win_ref, wtl_ref, lbp_ref, hn_ref,
                  cw_ref, rgp_ref, rwax_ref, wal_ref, bal_ref, gn_ref,
                  wbr_ref, wout_ref,
                  o_ref, sa_ref, cv_ref, hr_ref, sg_ref,
                  z0, z1, sta, stc, cvs, hrs, *, dims, tpb):
    d_a, d_b, dk_tot, d_c = dims
    s = pl.program_id(0)
    n, d = xc_ref.shape
    o_b = 4 * d_a
    o_c = o_b + 2 * d_b
    o_t = o_c + 2 * dk_tot + 2 * d_c
    o_end = o_t + 3 * d
    dk_c = dk_tot // H_C
    prev = s - 1

    @pl.when(s == 0)
    def _():
        z1[...] = jnp.zeros_like(z1)
        sta[...] = jnp.zeros_like(sta)
        stc[...] = jnp.zeros_like(stc)
        cvs[...] = jnp.zeros_like(cvs)
        hrs[...] = jnp.zeros_like(hrs)

    def step(zw, zr):
        h = _ada_norm(xc_ref[...], gain_ref[...], sh_ref[0], sc_ref[0]).astype(BF16)
        piece = 2 * LANES
        todo = list(range(0, o_end, piece)) + [o_end]

        credit = [0.0]

        def pump(count=PUMP_RATE):
            credit[0] += count
            while credit[0] >= 1.0:
                credit[0] -= 1.0
                if todo:
                    c0 = todo.pop(0)
                    if c0 < o_end:
                        zw[:, c0:c0 + piece] = _dot(h, win_ref[:, c0:c0 + piece])
                    else:
                        zw[:, c0:c0 + LANES] = _dot(h, wtl_ref[...])

        keep = jnp.where(lax.rem(prev, tpb) == 0, 0.0, 1.0).astype(F32)
        masks, eye, tri = _level_masks(REC_CHUNK)
        x = xp_ref[...]

        lbp = lbp_ref[...]
        pump(2.0)
        q, k, log_f = _hgrn_gates(zr[:, 0:d_a], zr[:, d_a:2 * d_a], lbp[0:1], lbp[1:2], lbp[2:3])
        pump(2.0)
        oa = _recurrence_tile(q, k, zr[:, 2 * d_a:3 * d_a], log_f, sta, keep,
                              H_A, REC_CHUNK, masks, eye, tri, pump)
        y_a = _head_out(oa, hn_ref[...], zr[:, 3 * d_a:4 * d_a])
        pump(2.0)

        bx = zr[:, o_b:o_b + d_b]
        cw = cw_ref[...]
        rgp = rgp_ref[...]
        ext = jnp.concatenate([cvs[...] * keep, bx], axis=0)
        conv = rgp[0:1] + cw[3:4] * bx
        for jj in range(CONV_W - 1):
            off = SUBLANES - (CONV_W - 1) + jj
            conv = conv + cw[jj:jj + 1] * ext[off:off + n]
        cvs[...] = bx[n - SUBLANES:n]
        pump(2.0)
        a, u = _rglru_gates(conv, rwax_ref, rgp[1:2], rgp[2:3], rgp[3:4])
        pump(2.0)
        hseq, hlast = _linear_scan(a, u, hrs[0:1] * keep, pump)
        hrs[...] = jnp.broadcast_to(hlast, hrs.shape)
        y_b = _gelu_tanh(zr[:, o_b + d_b:o_c]) * hseq

        log_fc = _gla_log_f(zr[:, o_t:o_t + LANES], wal_ref, bal_ref[...])
        oc = _recurrence_tile(zr[:, o_c:o_c + dk_tot] * (dk_c ** -0.5),
                              zr[:, o_c + dk_tot:o_c + 2 * dk_tot],
                              zr[:, o_c + 2 * dk_tot:o_c + 2 * dk_tot + d_c], log_fc, stc, keep,
                              H_C, REC_CHUNK, masks, eye, tri, pump)
        y_c = _head_out(oc, gn_ref[...], zr[:, o_c + 2 * dk_tot + d_c:o_t])
        pump(len(todo))

        gates = _rot_gates(zr[:, o_t:o_end + LANES], d)
        o_ref[...] = _merge(x, gt_ref[0], gates, y_a, y_b, y_c, wbr_ref, wout_ref)

    @pl.when(lax.rem(s, 2) == 0)
    def _():
        step(z0, z1)

    @pl.when(lax.rem(s, 2) == 1)
    def _():
        step(z1, z0)

    @pl.when(jnp.logical_and(s >= 1, lax.rem(prev, tpb) == tpb - 1))
    def _():
        for hh in range(H_A):
            sa_ref[hh] = sta[hh].T
        for hp in range(H_C // 2):
            pair = jnp.concatenate([stc[2 * hp], stc[2 * hp + 1]], axis=1).T
            sg_ref[2 * hp] = pair[0:dk_c]
            sg_ref[2 * hp + 1] = pair[dk_c:2 * dk_c]
        cv_ref[...] = cvs[SUBLANES - (CONV_W - 1):SUBLANES]
        hr_ref[...] = hrs[0:1]


def _mixer_call(x, mod4, gain, win, lbp, hnorm, cw, rgp, rwax, wal, bal, gnorm, wbr, wout,
                layer, batch, seq, dims):
    n, d = x.shape
    d_a, d_b, dk_tot, d_c = dims
    tile = min(MIX_TILE, seq)
    assert seq % tile == 0 and tile % REC_CHUNK == 0
    tpb = seq // tile
    dk_a, dv_a = d_a // H_A, d_a // H_A
    dk_c, dv_c = dk_tot // H_C, d_c // H_C
    nmain = 4 * d_a + 2 * d_b + 2 * dk_tot + 2 * d_c + 3 * d
    nin = nmain + LANES
    assert win.shape[2] == nin and nmain % (2 * LANES) == 0
    nt = batch * tpb
    cur = lambda s: jnp.minimum(s, nt - 1)
    prv = lambda s: jnp.maximum(s - 1, 0)
    mspec = lambda col, t: pl.BlockSpec((None, 1, 1, d), lambda s: (layer, t(s) // tpb, 0, col))
    lsel = lambda *rest: (lambda s: (layer,) + rest)
    in_specs = [
        pl.BlockSpec((tile, d), lambda s: (cur(s), 0)),
        pl.BlockSpec((tile, d), lambda s: (prv(s), 0)),
        mspec(3, cur), mspec(4, cur), mspec(5, prv),
        pl.BlockSpec((None, None, 1, d), lsel(1, 0, 0)),
        pl.BlockSpec((None, d, nmain), lsel(0, 0), pipeline_mode=pl.Buffered(1)),
        pl.BlockSpec((None, d, LANES), lsel(0, nmain // LANES)),
        pl.BlockSpec((None, 3, d_a), lsel(0, 0)),
        pl.BlockSpec((None, 1, dv_a), lsel(0, 0)),
        pl.BlockSpec((None, CONV_W, d_b), lsel(0, 0)),
        pl.BlockSpec((None, 4, d_b), lsel(0, 0)),
        pl.BlockSpec((None, NB_B, d_b // NB_B, 2 * d_b // NB_B), lsel(0, 0, 0)),
        pl.BlockSpec((None, LANES, dk_tot), lsel(0, 0)),
        pl.BlockSpec((None, 1, dk_tot), lsel(0, 0)),
        pl.BlockSpec((None, 1, dv_c), lsel(0, 0)),
        pl.BlockSpec((None, 3, d_a, d), lsel(0, 0, 0), pipeline_mode=pl.Buffered(1)),
        pl.BlockSpec((None, d, d), lsel(0, 0), pipeline_mode=pl.Buffered(1)),
    ]
    out_specs = [
        pl.BlockSpec((tile, d), lambda s: (prv(s), 0)),
        pl.BlockSpec((None, H_A, dk_a, dv_a), lambda s: (prv(s) // tpb, 0, 0, 0)),
        pl.BlockSpec((None, CONV_W - 1, d_b), lambda s: (prv(s) // tpb, 0, 0)),
        pl.BlockSpec((None, 1, d_b), lambda s: (prv(s) // tpb, 0, 0)),
        pl.BlockSpec((None, H_C, dk_c, dv_c), lambda s: (prv(s) // tpb, 0, 0, 0)),
    ]
    out_shape = [
        jax.ShapeDtypeStruct((n, d), F32),
        jax.ShapeDtypeStruct((batch, H_A, dk_a, dv_a), F32),
        jax.ShapeDtypeStruct((batch, CONV_W - 1, d_b), F32),
        jax.ShapeDtypeStruct((batch, 1, d_b), F32),
        jax.ShapeDtypeStruct((batch, H_C, dk_c, dv_c), F32),
    ]
    scratch = [
        pltpu.VMEM((tile, nin), F32),
        pltpu.VMEM((tile, nin), F32),
        pltpu.VMEM((H_A, dv_a, dk_a), F32),
        pltpu.VMEM((H_C, dv_c, dk_c), F32),
        pltpu.VMEM((SUBLANES, d_b), F32),
        pltpu.VMEM((SUBLANES, d_b), F32),
    ]
    return pl.pallas_call(
        functools.partial(_mixer_kernel, dims=dims, tpb=tpb),
        grid=(nt + 1,),
        in_specs=in_specs,
        out_specs=out_specs,
        out_shape=out_shape,
        scratch_shapes=scratch,
        compiler_params=pltpu.CompilerParams(
            dimension_semantics=("arbitrary",), vmem_limit_bytes=VMEM_LIMIT_BYTES),
        name="mixer_prompt",
    )(x, x, mod4, mod4, mod4, gain, win, win, lbp, hnorm, cw, rgp, rwax, wal, bal,
      gnorm, wbr, wout)


def _dec_pre_kernel(x_ref, sh_ref, sc_ref, gain_ref, win_ref, wtl_ref, lbp_ref, cw_ref, rgp_ref,
                    rwax_ref, wal_ref, bal_ref, cvin_ref, hrin_ref,
                    sv_ref, post_ref, cvout_ref, hrout_ref, *, dims):
    d_a, d_b, dk_tot, d_c = dims
    x = x_ref[...]
    d = x.shape[1]
    h = _ada_norm(x, gain_ref[...], sh_ref[0], sc_ref[0]).astype(BF16)
    o_b = 4 * d_a
    o_c = o_b + 2 * d_b
    o_t = o_c + 2 * dk_tot + 2 * d_c
    o_end = o_t + 3 * d

    za = _dot(h, win_ref[:, 0:o_b])
    lbp = lbp_ref[...]
    q, k, log_f = _hgrn_gates(za[:, 0:d_a], za[:, d_a:2 * d_a], lbp[0:1], lbp[1:2], lbp[2:3])

    zb = _dot(h, win_ref[:, o_b:o_c])
    bx = zb[:, 0:d_b]
    cw = cw_ref[...]
    rgp = rgp_ref[...]
    cvin = cvin_ref[...]
    conv = rgp[0:1] + cw[3:4] * bx
    for jj in range(CONV_W - 1):
        conv = conv + cw[jj:jj + 1] * cvin[:, jj * d_b:(jj + 1) * d_b]
    a, u = _rglru_gates(conv, rwax_ref, rgp[1:2], rgp[2:3], rgp[3:4])
    hnew = u + a * hrin_ref[...]
    y_b = _gelu_tanh(zb[:, d_b:2 * d_b]) * hnew
    cvout_ref[...] = jnp.concatenate([cvin[:, d_b:(CONV_W - 1) * d_b], bx], axis=-1)
    hrout_ref[...] = hnew

    zc = _dot(h, win_ref[:, o_c:o_t])
    zt = jnp.concatenate([_dot(h, win_ref[:, o_t:o_end]), _dot(h, wtl_ref[...])], axis=-1)
    log_fc = _gla_log_f(zt[:, 0:LANES], wal_ref, bal_ref[...])
    dk_c = dk_tot // H_C

    sv_ref[...] = jnp.concatenate(
        [q, jnp.exp(log_f), k, za[:, 2 * d_a:3 * d_a],
         zc[:, 0:dk_tot] * (dk_c ** -0.5), jnp.exp(log_fc), zc[:, dk_tot:2 * dk_tot],
         zc[:, 2 * dk_tot:2 * dk_tot + d_c]], axis=-1)
    post_ref[...] = jnp.concatenate(
        [za[:, 3 * d_a:4 * d_a], y_b, zc[:, 2 * dk_tot + d_c:2 * dk_tot + 2 * d_c]]
        + _rot_gates(zt, d), axis=-1)


def _dec_pre_call(x, mod4, gain, win, lbp, cw, rgp, rwax, wal, bal, cv_all, hr_all, layer, dims):
    n, d = x.shape
    d_a, d_b, dk_tot, d_c = dims
    nmain = win.shape[2] - LANES
    sv_w = 4 * d_a + 3 * dk_tot + d_c
    post_w = d_a + d_b + d_c + 3 * d
    mspec = lambda col: pl.BlockSpec((None, 1, n, d), lambda i: (layer, 0, 0, col))
    lsel = lambda *rest: (lambda i: (layer,) + rest)
    in_specs = [
        pl.BlockSpec((n, d), lambda i: (0, 0)),
        mspec(3), mspec(4),
        pl.BlockSpec((None, None, 1, d), lsel(1, 0, 0)),
        pl.BlockSpec((None, d, nmain), lsel(0, 0), pipeline_mode=pl.Buffered(1)),
        pl.BlockSpec((None, d, LANES), lsel(0, nmain // LANES)),
        pl.BlockSpec((None, 3, d_a), lsel(0, 0)),
        pl.BlockSpec((None, CONV_W, d_b), lsel(0, 0)),
        pl.BlockSpec((None, 4, d_b), lsel(0, 0)),
        pl.BlockSpec((None, NB_B, d_b // NB_B, 2 * d_b // NB_B), lsel(0, 0, 0)),
        pl.BlockSpec((None, LANES, dk_tot), lsel(0, 0)),
        pl.BlockSpec((None, 1, dk_tot), lsel(0, 0)),
        pl.BlockSpec((None, n, (CONV_W - 1) * d_b), lsel(0, 0)),
        pl.BlockSpec((None, n, d_b), lsel(0, 0)),
    ]
    full = lambda w: pl.BlockSpec((n, w), lambda i: (0, 0))
    return pl.pallas_call(
        functools.partial(_dec_pre_kernel, dims=dims),
        grid=(1,),
        in_specs=in_specs,
        out_specs=[full(sv_w), full(post_w), full((CONV_W - 1) * d_b), full(d_b)],
        out_shape=[jax.ShapeDtypeStruct((n, sv_w), F32), jax.ShapeDtypeStruct((n, post_w), F32),
                   jax.ShapeDtypeStruct((n, (CONV_W - 1) * d_b), F32),
                   jax.ShapeDtypeStruct((n, d_b), F32)],
        compiler_params=pltpu.CompilerParams(
            dimension_semantics=("arbitrary",), vmem_limit_bytes=VMEM_LIMIT_BYTES),
        name="mixer_sample_pre",
    )(x, mod4, mod4, gain, win, win, lbp, cw, rgp, rwax, wal, bal, cv_all, hr_all)


def _columns(rows):
    pad = jnp.zeros((LANES - rows.shape[0], LANES), F32)
    return jnp.concatenate([rows, pad], axis=0).T


def _dec_state_kernel(sv_ref, sa_ref, sg_ref, *rest, dims):
    oa_ref, oc_ref, sa_out, sg_out = rest[-4:]
    d_a, d_b, dk_tot, d_c = dims
    sv = sv_ref[...]
    g = sv.shape[0]
    dk_a = d_a // H_A
    dk_c = dk_tot // H_C
    dv_c = d_c // H_C
    o_q, o_f, o_k, o_v = 0, d_a, 2 * d_a, 3 * d_a
    for hh in range(H_A):
        qt = _columns(sv[:, o_q + hh * dk_a:o_q + (hh + 1) * dk_a])
        ft = _columns(sv[:, o_f + hh * dk_a:o_f + (hh + 1) * dk_a])
        kt = _columns(sv[:, o_k + hh * dk_a:o_k + (hh + 1) * dk_a])
        for jj in range(g):
            v = sv[jj:jj + 1, o_v + hh * dk_a:o_v + (hh + 1) * dk_a]
            s_new = ft[:, jj:jj + 1] * sa_ref[jj, hh] + kt[:, jj:jj + 1] * v
            sa_out[jj, hh] = s_new
            oa_ref[jj:jj + 1, hh * dk_a:(hh + 1) * dk_a] = jnp.sum(
                qt[:, jj:jj + 1] * s_new, axis=0, keepdims=True)
    c_q = 4 * d_a
    c_f = c_q + dk_tot
    c_k = c_f + dk_tot
    c_v = c_k + dk_tot
    per_tile = LANES // dk_c
    for tp in range(dk_tot // LANES):
        qt = _columns(sv[:, c_q + tp * LANES:c_q + (tp + 1) * LANES])
        ft = _columns(sv[:, c_f + tp * LANES:c_f + (tp + 1) * LANES])
        kt = _columns(sv[:, c_k + tp * LANES:c_k + (tp + 1) * LANES])
        for hp in range(per_tile):
            hh = tp * per_tile + hp
            rs = slice(hp * dk_c, (hp + 1) * dk_c)
            for jj in range(g):
                v = sv[jj:jj + 1, c_v + hh * dv_c:c_v + (hh + 1) * dv_c]
                s_new = ft[rs, jj:jj + 1] * sg_ref[jj, hh] + kt[rs, jj:jj + 1] * v
                sg_out[jj, hh] = s_new
                oc_ref[jj:jj + 1, hh * dv_c:(hh + 1) * dv_c] = jnp.sum(
                    qt[rs, jj:jj + 1] * s_new, axis=0, keepdims=True)


def _dec_state_call(sv, st_a, st_c, layer, dims, stacked):
    n, sv_w = sv.shape
    d_a, d_b, dk_tot, d_c = dims
    depth = st_a.shape[0]
    g = DEC_GROUP
    assert n % g == 0
    dk_a = d_a // H_A
    dk_c, dv_c = dk_tot // H_C, d_c // H_C
    in_specs = [
        pl.BlockSpec((g, sv_w), lambda i: (i, 0)),
        pl.BlockSpec((None, g, H_A, dk_a, dk_a), lambda i: (layer, i, 0, 0, 0)),
        pl.BlockSpec((None, g, H_C, dk_c, dv_c), lambda i: (layer, i, 0, 0, 0)),
    ]
    in_specs += [pl.BlockSpec(memory_space=pl.ANY), pl.BlockSpec(memory_space=pl.ANY)]
    args = [sv, st_a, st_c, *stacked]
    aliases = {3: 2, 4: 3}
    return pl.pallas_call(
        functools.partial(_dec_state_kernel, dims=dims),
        grid=(n // g,),
        in_specs=in_specs,
        out_specs=[
            pl.BlockSpec((g, d_a), lambda i: (i, 0)),
            pl.BlockSpec((g, d_c), lambda i: (i, 0)),
            pl.BlockSpec((None, g, H_A, dk_a, dk_a), lambda i: (layer, i, 0, 0, 0)),
            pl.BlockSpec((None, g, H_C, dk_c, dv_c), lambda i: (layer, i, 0, 0, 0)),
        ],
        out_shape=[
            jax.ShapeDtypeStruct((n, d_a), F32),
            jax.ShapeDtypeStruct((n, d_c), F32),
            jax.ShapeDtypeStruct((depth, n, H_A, dk_a, dk_a), F32),
            jax.ShapeDtypeStruct((depth, n, H_C, dk_c, dv_c), F32),
        ],
        input_output_aliases=aliases,
        compiler_params=pltpu.CompilerParams(
            dimension_semantics=("arbitrary",), vmem_limit_bytes=VMEM_LIMIT_BYTES),
        name="mixer_sample_state",
    )(*args)


def _dec_post_kernel(x_ref, gt_ref, post_ref, oa_ref, oc_ref, hn_ref, gn_ref, wbr_ref, wout_ref,
                     o_ref, *, dims):
    d_a, d_b, dk_tot, d_c = dims
    x = x_ref[...]
    post = post_ref[...]
    oa = oa_ref[...]
    oc = oc_ref[...]
    dv_a = d_a // H_A
    dv_c = d_c // H_C
    y_a = _head_out([oa[:, hh * dv_a:(hh + 1) * dv_a] for hh in range(H_A)], hn_ref[...],
                    post[:, 0:d_a])
    y_b = post[:, d_a:d_a + d_b]
    y_c = _head_out([oc[:, hh * dv_c:(hh + 1) * dv_c] for hh in range(H_C)], gn_ref[...],
                    post[:, d_a + d_b:d_a + d_b + d_c])
    off = d_a + d_b + d_c
    d = x.shape[1]
    gates = [post[:, off + i * d:off + (i + 1) * d] for i in range(3)]
    o_ref[...] = _merge(x, gt_ref[0], gates, y_a, y_b, y_c, wbr_ref, wout_ref)


def _dec_post_call(x, mod4, post, oa, oc, hnorm, gnorm, wbr, wout, layer, dims):
    n, d = x.shape
    d_a, d_b, dk_tot, d_c = dims
    lsel = lambda *rest: (lambda i: (layer,) + rest)
    full = lambda w: pl.BlockSpec((n, w), lambda i: (0, 0))
    return pl.pallas_call(
        functools.partial(_dec_post_kernel, dims=dims),
        grid=(1,),
        in_specs=[
            full(d),
            pl.BlockSpec((None, 1, n, d), lambda i: (layer, 0, 0, 5)),
            full(post.shape[1]), full(d_a), full(d_c),
            pl.BlockSpec((None, 1, d_a // H_A), lsel(0, 0)),
            pl.BlockSpec((None, 1, d_c // H_C), lsel(0, 0)),
            pl.BlockSpec((None, 3, d_a, d), lsel(0, 0, 0), pipeline_mode=pl.Buffered(1)),
            pl.BlockSpec((None, d, d), lsel(0, 0), pipeline_mode=pl.Buffered(1)),
        ],
        out_specs=full(d),
        out_shape=jax.ShapeDtypeStruct((n, d), F32),
        compiler_params=pltpu.CompilerParams(
            dimension_semantics=("arbitrary",), vmem_limit_bytes=VMEM_LIMIT_BYTES),
        name="mixer_sample_post",
    )(x, mod4, post, oa, oc, hnorm, gnorm, wbr, wout)


def kernel(x_prompt, x_sample, c_prompt, c_sample, state_hgrn, state_conv, state_rglru, state_gla,
           w_ada, b_ada, norm_gain, w_ffn_in, w_ffn_out, w_in, hgrn_lb_logits, hgrn_norm,
           conv_w, conv_b, rg_wa, rg_ba, rg_wx, rg_bx, rg_lambda, gla_w_alpha, gla_b_alpha,
           gla_norm, w_br_a, w_br_b, w_br_c, w_out, final_norm):
    batch, seq, d = x_prompt.shape
    nd = x_sample.shape[0]
    depth = w_in.shape[0]
    d_a = hgrn_lb_logits.shape[1]
    d_b = conv_b.shape[1]
    dk_tot = gla_b_alpha.shape[1]
    d_c = w_br_c.shape[1]
    dims = (d_a, d_b, dk_tot, d_c)
    nm = w_ada.shape[2]

    wi = w_ffn_in.astype(BF16)
    wo = w_ffn_out.astype(BF16)
    o_ca = 4 * d_a + 2 * d_b + 2 * dk_tot + 2 * d_c
    assert w_in.shape[2] == o_ca + 3 * d + GATE_RANK
    win = jnp.pad(w_in, ((0, 0), (0, 0), (0, LANES - GATE_RANK))).astype(BF16)
    wal = jnp.concatenate(
        [gla_w_alpha, jnp.zeros((depth, LANES - GATE_RANK, dk_tot), gla_w_alpha.dtype)],
        axis=1).astype(BF16)
    bal = gla_b_alpha.reshape(depth, 1, dk_tot)
    wbr = jnp.roll(jnp.stack([w_br_a, w_br_b, w_br_c], axis=1), GATE_RANK, axis=-1).astype(BF16)
    wout = jnp.roll(w_out, GATE_RANK, axis=1).astype(BF16)
    rwax = jnp.concatenate([rg_wa, rg_wx], axis=-1).astype(BF16)
    rgp = jnp.stack([conv_b, rg_ba, rg_bx, rg_lambda], axis=1)
    hnorm = hgrn_norm.reshape(depth, 1, -1)
    gnorm = gla_norm.reshape(depth, 1, -1)
    lb = jnp.cumsum(jax.nn.softmax(hgrn_lb_logits.astype(F32), axis=0), axis=0)
    lb = lb - lb[0]
    lbp = jnp.stack([jnp.log(lb), jnp.log1p(-lb), 1.0 - lb], axis=1)

    norm_gain = norm_gain.reshape(depth, N_SUB, 1, d)
    c_all = jnp.concatenate([c_prompt, c_sample], axis=0)
    mod = _ada_call(c_all, w_ada, b_ada)
    mod_p = mod[:, :batch].reshape(depth, batch, 1, nm)
    mod_s = mod[:, batch:].reshape(depth, 1, nd, nm)

    ffn_tile = min(FFN_TILE, seq)
    tpg = seq // ffn_tile
    x = x_prompt.reshape(batch * seq, d)
    sa_p, cv_p, hr_p, sg_p = [], [], [], []
    zero_a = zero_c = None
    for l in range(depth):
        x = _ffn_call(x, mod_p, norm_gain, wi, wo, l, 0, 0, ffn_tile, tpg,
                      fill_shape=state_hgrn.shape if l == 0 else None)
        if l == 0:
            x, zero_a = x
        x, sa, cv, hr, sg = _mixer_call(x, mod_p, norm_gain, win, lbp, hnorm, conv_w, rgp, rwax,
                                        wal, bal, gnorm, wbr, wout, l, batch, seq, dims)
        x = _ffn_call(x, mod_p, norm_gain, wi, wo, l, 1, 2, ffn_tile, tpg,
                      final_gain=final_norm if l == depth - 1 else None,
                      fill_shape=state_gla.shape if l == 0 else None)
        if l == 0:
            x, zero_c = x
        sa_p.append(sa); cv_p.append(cv); hr_p.append(hr[:, 0]); sg_p.append(sg)
    y_prompt = x.reshape(batch, seq, d)

    xs = x_sample.reshape(nd, d)
    cv_all = state_conv.reshape(depth, nd, (CONV_W - 1) * d_b)
    cv_s, hr_s = [], []
    stacked = (zero_a, zero_c)
    for l in range(depth):
        xs = _ffn_call(xs, mod_s, norm_gain, wi, wo, l, 0, 0, nd, 1)
        sv, post, cv, hr = _dec_pre_call(xs, mod_s, norm_gain, win, lbp, conv_w, rgp, rwax,
                                         wal, bal, cv_all, state_rglru, l, dims)
        oa, oc, sa_s, sg_s = _dec_state_call(sv, state_hgrn, state_gla, l, dims, stacked)
        stacked = (sa_s, sg_s)
        xs = _dec_post_call(xs, mod_s, post, oa, oc, hnorm, gnorm, wbr, wout, l, dims)
        xs = _ffn_call(xs, mod_s, norm_gain, wi, wo, l, 1, 2, nd, 1,
                       final_gain=final_norm if l == depth - 1 else None)
        cv_s.append(cv.reshape(nd, CONV_W - 1, d_b)); hr_s.append(hr)
    y_sample = xs.reshape(nd, 1, d)

    st = jnp.stack
    return (y_prompt, y_sample, st(sa_p), sa_s, st(cv_p), st(cv_s), st(hr_p), st(hr_s),
            st(sg_p), sg_s)
```

```python
import functools
import math

import jax
import jax.numpy as jnp
from jax import lax
from jax.experimental import pallas as pl
from jax.experimental.pallas import tpu as pltpu

F32 = jnp.float32
BF16 = jnp.bfloat16

EPS = 1e-6
H_A = 4
NB_B = 4
CONV_W = 4
RG_C = 8.0
H_C = 4
GATE_RANK = 16
GLA_TAU = 16.0
N_SUB = 3
LOG2E = 1.4426950408889634

LANES = 128
SUBLANES = 8
VMEM_LIMIT_BYTES = 56 * 1024 * 1024

REC_CHUNK = 128
MIX_TILE = 256
FFN_TILE = 512
DEC_GROUP = 8
ADA_TILE = 1536


def _dot(a, b):
    return jnp.dot(a, b, preferred_element_type=F32)


def _dot_nt(a, b):
    return lax.dot_general(a, b, (((1,), (1,)), ((), ())), preferred_element_type=F32)


def _dot_tn(a, b):
    return lax.dot_general(a, b, (((0,), (0,)), ((), ())), preferred_element_type=F32)


def _sigmoid(x):
    return 0.5 * jnp.tanh(0.5 * x) + 0.5


def _silu(x):
    h = 0.5 * x
    return h * jnp.tanh(h) + h


def _softplus(x):
    return jnp.maximum(x, 0.0) + jnp.log(1.0 + jnp.exp(-jnp.abs(x)))


def _log_sigmoid(x):
    return -_softplus(-x)


def _gelu_tanh(x):
    c = math.sqrt(2.0 / math.pi)
    return x * (0.5 * (1.0 + jnp.tanh(c * (x + 0.044715 * (x * x * x)))))


def _rms(x, gain):
    ms = jnp.mean(x * x, axis=-1, keepdims=True)
    return x * lax.rsqrt(ms + EPS) * gain


def _ada_norm(x, gain, shift, scale):
    return _rms(x, gain) * (1.0 + scale) + shift


def _ada_kernel(c_ref, w_ref, b_ref, o_ref):
    o_ref[...] = _dot(c_ref[...].astype(BF16), w_ref[...].astype(BF16)) + b_ref[...]


def _ada_call(c_all, w_ada, b_ada):
    depth, d, nm = w_ada.shape
    rows = c_all.shape[0]
    tn = ADA_TILE
    assert nm % tn == 0
    return pl.pallas_call(
        _ada_kernel,
        grid=(depth, nm // tn),
        in_specs=[
            pl.BlockSpec((rows, d), lambda l, n: (0, 0)),
            pl.BlockSpec((None, d, tn), lambda l, n: (l, 0, n)),
            pl.BlockSpec((None, 1, tn), lambda l, n: (l, 0, n)),
        ],
        out_specs=pl.BlockSpec((None, rows, tn), lambda l, n: (l, 0, n)),
        out_shape=jax.ShapeDtypeStruct((depth, rows, nm), F32),
        compiler_params=pltpu.CompilerParams(
            dimension_semantics=("arbitrary", "arbitrary"),
            vmem_limit_bytes=VMEM_LIMIT_BYTES),
        name="ada_mod",
    )(c_all, w_ada, b_ada.reshape(depth, 1, nm))


def _ffn_kernel(x_ref, sh_ref, sc_ref, gt_ref, gain_ref, wi_ref, wo_ref, *rest, n_chunks, final, fill):
    rest = list(rest)
    fg_ref = rest.pop(0) if final else None
    o_ref = rest.pop(0)
    if fill:
        zero_ref = rest.pop(0)
        zero_ref[...] = jnp.zeros_like(zero_ref)
    x = x_ref[...]
    h = _ada_norm(x, gain_ref[...], sh_ref[0], sc_ref[0]).astype(BF16)
    f = wo_ref.shape[0]
    fc = f // n_chunks
    acc = None
    for c in range(n_chunks):
        g = _dot(h, wi_ref[:, c * fc:(c + 1) * fc])
        u = _dot(h, wi_ref[:, f + c * fc:f + (c + 1) * fc])
        a = (_silu(g) * u).astype(BF16)
        part = _dot(a, wo_ref[c * fc:(c + 1) * fc, :])
        acc = part if acc is None else acc + part
    y = x + (0.5 * gt_ref[0]) * acc
    if final:
        y = _rms(y, fg_ref[...])
    o_ref[...] = y


def _mod_spec(rows_per_group, d, layer, col, group_of):
    return pl.BlockSpec((None, 1, rows_per_group, d), lambda i: (layer, group_of(i), 0, col))


def _ffn_call(x, mod4, gain, wi, wo, layer, which, sub, tile, tiles_per_group, final_gain=None,
              fill_shape=None):
    n, d = x.shape
    f = wo.shape[2]
    r = 1 if mod4.shape[2] == 1 else tile
    group_of = (lambda i: i // tiles_per_group)
    final = final_gain is not None
    in_specs = [
        pl.BlockSpec((tile, d), lambda i: (i, 0)),
        _mod_spec(r, d, layer, 3 * sub + 0, group_of),
        _mod_spec(r, d, layer, 3 * sub + 1, group_of),
        _mod_spec(r, d, layer, 3 * sub + 2, group_of),
        pl.BlockSpec((None, None, 1, d), lambda i: (layer, sub, 0, 0)),
        pl.BlockSpec((None, None, d, 2 * f), lambda i: (layer, which, 0, 0),
                     pipeline_mode=pl.Buffered(1)),
        pl.BlockSpec((None, None, f, d), lambda i: (layer, which, 0, 0),
                     pipeline_mode=pl.Buffered(1)),
    ]
    args = [x, mod4, mod4, mod4, gain, wi, wo]
    if final:
        in_specs.append(pl.BlockSpec((1, d), lambda i: (0, 0)))
        args.append(final_gain.reshape(1, d))
    steps = n // tile
    out_specs = [pl.BlockSpec((tile, d), lambda i: (i, 0))]
    out_shape = [jax.ShapeDtypeStruct((n, d), F32)]
    if fill_shape is not None:
        lead, second = fill_shape[0], fill_shape[1]
        assert steps % lead == 0 and second % (steps // lead) == 0
        per_lead = steps // lead
        blk = second // per_lead
        out_specs.append(pl.BlockSpec((None, blk) + tuple(fill_shape[2:]),
                                      lambda i: (i // per_lead, i % per_lead, 0, 0, 0)))
        out_shape.append(jax.ShapeDtypeStruct(tuple(fill_shape), F32))
    out = pl.pallas_call(
        functools.partial(_ffn_kernel, n_chunks=1, final=final, fill=fill_shape is not None),
        grid=(steps,),
        in_specs=in_specs,
        out_specs=out_specs,
        out_shape=out_shape,
        compiler_params=pltpu.CompilerParams(
            dimension_semantics=("arbitrary",), vmem_limit_bytes=VMEM_LIMIT_BYTES),
        name="ffn",
    )(*args)
    return tuple(out) if fill_shape is not None else out[0]


def _hgrn_gates(a_q, a_f, loglb, log1mlb, omlb):
    q = _silu(a_q)
    c = log1mlb + _log_sigmoid(a_f)
    m = jnp.maximum(loglb, c)
    log_f = m + jnp.log(1.0 + jnp.exp(-jnp.abs(loglb - c)))
    k = omlb * _sigmoid(-a_f)
    return q, k, log_f


def _gla_log_f(c_a, wal_ref, bal):
    return _log_sigmoid(_dot(c_a.astype(BF16), wal_ref[...]) + bal) * (1.0 / GLA_TAU)


def _rglru_gates(conv, wax_ref, ba, bx, lam):
    cb = conv.astype(BF16)
    bs = conv.shape[1] // NB_B
    ri = [_dot(cb[:, n * bs:(n + 1) * bs], wax_ref[n]) for n in range(NB_B)]
    r = _sigmoid(jnp.concatenate([p[:, 0:bs] for p in ri], axis=-1) + ba)
    i = _sigmoid(jnp.concatenate([p[:, bs:2 * bs] for p in ri], axis=-1) + bx)
    log_a = (-RG_C) * r * _softplus(-lam)
    a = jnp.exp(log_a)
    u = jnp.sqrt(1.0 - jnp.exp(2.0 * log_a)) * i * conv
    return a, u


def _head_out(o_heads, gain, gate):
    return jnp.concatenate([_rms(o, gain) for o in o_heads], axis=-1) * _silu(gate)


def _rot_gates(zt, d):
    lane = lax.broadcasted_iota(jnp.int32, (zt.shape[0], LANES), 1)
    out = []
    for i in range(3):
        first = jnp.where(lane < GATE_RANK, zt[:, (i + 1) * d:(i + 1) * d + LANES],
                          zt[:, i * d:i * d + LANES])
        out.append(jnp.concatenate([first, zt[:, i * d + LANES:(i + 1) * d]], axis=1))
    return out


def _merge(x, gate, gates, y_a, y_b, y_c, wbr_ref, wout_ref):
    merged = (_sigmoid(gates[0]) * _dot(y_a.astype(BF16), wbr_ref[0])
              + _sigmoid(gates[1]) * _dot(y_b.astype(BF16), wbr_ref[1])
              + _sigmoid(gates[2]) * _dot(y_c.astype(BF16), wbr_ref[2]))
    return x + gate * _dot(merged.astype(BF16), wout_ref[...])


PUMP_RATE = 1.0


def _no_pump(count=PUMP_RATE):
    del count


def _split3(x):
    hi = x.astype(BF16)
    r1 = x - hi.astype(F32)
    mid = r1.astype(BF16)
    lo = (r1 - mid.astype(F32)).astype(BF16)
    return hi, mid, lo


def _chunk_cumsum(g, chunk, tri):
    n = g.shape[0]
    hi, mid, lo = _split3(g)
    out = []
    for c in range(n // chunk):
        rs = slice(c * chunk, (c + 1) * chunk)
        out.append(_dot(tri, lo[rs]) + _dot(tri, mid[rs]) + _dot(tri, hi[rs]))
    return jnp.concatenate(out, axis=0) if len(out) > 1 else out[0]


def _level_operands(q, k, g, chunk, tri, pump=_no_pump):
    n, w = g.shape
    g = g * LOG2E
    rin = lax.broadcasted_iota(jnp.int32, (n, w), 0) & (chunk - 1)
    b = _chunk_cumsum(g, chunk, tri)
    xs = []
    n_levels = chunk.bit_length() - 1
    for l in range(n_levels):
        m = 1 << l
        right = (rin & m) != 0
        if l == 0:
            dlt = jnp.where(right, g, 0.0)
        elif l == 1:
            gm1 = pltpu.roll(g, 1, 0)
            gp1 = pltpu.roll(g, n - 1, 0)
            r4 = rin & 3
            dlt = jnp.where(r4 == 0, gp1, jnp.where(r4 == 1, 0.0, jnp.where(r4 == 2, g, g + gm1)))
        else:
            p = 2 * m
            b3 = b.reshape(n // p, p, w)
            ref = jnp.broadcast_to(b3[:, m - 1:m, :], (n // p, p, w)).reshape(n, w)
            dlt = -jnp.abs(b - ref)
        xs.append((jnp.where(right, q, k) * jnp.exp2(dlt)).astype(BF16))
        pump()
    b3 = b.reshape(n // chunk, chunk, w)
    bend3 = b3[:, chunk - 1:chunk, :]
    bend = jnp.broadcast_to(bend3, (n // chunk, chunk, w)).reshape(n, w)
    qe = (q * jnp.exp2(b)).astype(BF16)
    kd = (k * jnp.exp2(bend - b)).astype(BF16)
    ebend = jnp.exp2(bend3)
    return qe, kd, ebend, xs


def _level_masks(chunk):
    ri = lax.broadcasted_iota(jnp.int32, (chunk, chunk), 0)
    ci = lax.broadcasted_iota(jnp.int32, (chunk, chunk), 1)
    masks = []
    for l in range(chunk.bit_length() - 1):
        m = 1 << l
        same_parent = (ri >> (l + 1)) == (ci >> (l + 1))
        masks.append(same_parent & ((ri & m) != 0) & ((ci & m) == 0))
    tri = jnp.where(ri >= ci, 1.0, 0.0).astype(BF16)
    return masks, ri == ci, tri


def _recurrence_tile(q, k, v, g, st_ref, keep, n_heads, chunk, masks, eye, tri, pump=_no_pump):
    n = q.shape[0]
    dk = q.shape[1] // n_heads
    dv = v.shape[1] // n_heads
    qe, kd, ebend, xs = _level_operands(q, k, g, chunk, tri, pump)
    vb = v.astype(BF16)
    qk = q * k
    outs = []
    for h in range(n_heads):
        ks = slice(h * dk, (h + 1) * dk)
        vs = slice(h * dv, (h + 1) * dv)
        st = st_ref[h] * keep
        o_chunks = []
        for c in range(n // chunk):
            rs = slice(c * chunk, (c + 1) * chunk)
            dq = jnp.sum(qk[rs, ks], axis=-1, keepdims=True)
            scores = jnp.where(eye, dq, 0.0)
            for l, x in enumerate(xs):
                xl = x[rs, ks]
                scores = jnp.where(masks[l], _dot_nt(xl, xl), scores)
            vc = vb[rs, vs]
            o = _dot_nt(qe[rs, ks], st.astype(BF16)) + _dot(scores.astype(BF16), vc)
            o_chunks.append(o)
            st = st * ebend[c][:, ks] + _dot_tn(vc, kd[rs, ks])
        st_ref[h] = st
        outs.append(jnp.concatenate(o_chunks, axis=0) if len(o_chunks) > 1 else o_chunks[0])
    return outs


def _compose_scan(a, u, idx, axis, length, pump):
    s = 1
    while s < length:
        valid = idx >= s
        a_sh = pltpu.roll(a, s, axis)
        u_sh = pltpu.roll(u, s, axis)
        u = jnp.where(valid, a * u_sh + u, u)
        a = jnp.where(valid, a * a_sh, a)
        pump()
        s *= 2
    return a, u


def _linear_scan(a, u, h0, pump=_no_pump):
    n = a.shape[0]
    a_cum, u_cum = _compose_scan(a, u, lax.broadcasted_iota(jnp.int32, a.shape, 0), 0, n, pump)
    h = u_cum + a_cum * h0
    return h, h[n - 1:n]


def _mixer_kernel(xc_ref, xp_ref, sh_ref, sc_ref, gt_ref, gain_ref, win_ref, wtl_ref, lbp_ref, hn_ref,
                  cw_ref, rgp_ref, rwax_ref, wal_ref, bal_ref, gn_ref,
                  wbr_ref, wout_ref,
                  o_ref, sa_ref, cv_ref, hr_ref, sg_ref,
                  z0, z1, sta, stc, cvs, hrs, *, dims, tpb):
    d_a, d_b, dk_tot, d_c = dims
    s = pl.program_id(0)
    n, d = xc_ref.shape
    o_b = 4 * d_a
    o_c = o_b + 2 * d_b
    o_t = o_c + 2 * dk_tot + 2 * d_c
    o_end = o_t + 3 * d
    dk_c = dk_tot // H_C
    prev = s - 1

    @pl.when(s == 0)
    def _():
        z1[...] = jnp.zeros_like(z1)
        sta[...] = jnp.zeros_like(sta)
        stc[...] = jnp.zeros_like(stc)
        cvs[...] = jnp.zeros_like(cvs)
        hrs[...] = jnp.zeros_like(hrs)

    def step(zw, zr):
        h = _ada_norm(xc_ref[...], gain_ref[...], sh_ref[0], sc_ref[0]).astype(BF16)
        piece = 2 * LANES
        todo = list(range(0, o_end, piece)) + [o_end]

        credit = [0.0]

        def pump(count=PUMP_RATE):
            credit[0] += count
            while credit[0] >= 1.0:
                credit[0] -= 1.0
                if todo:
                    c0 = todo.pop(0)
                    if c0 < o_end:
                        zw[:, c0:c0 + piece] = _dot(h, win_ref[:, c0:c0 + piece])
                    else:
                        zw[:, c0:c0 + LANES] = _dot(h, wtl_ref[...])

        keep = jnp.where(lax.rem(prev, tpb) == 0, 0.0, 1.0).astype(F32)
        masks, eye, tri = _level_masks(REC_CHUNK)
        x = xp_ref[...]

        lbp = lbp_ref[...]
        pump(2.0)
        q, k, log_f = _hgrn_gates(zr[:, 0:d_a], zr[:, d_a:2 * d_a], lbp[0:1], lbp[1:2], lbp[2:3])
        pump(2.0)
        oa = _recurrence_tile(q, k, zr[:, 2 * d_a:3 * d_a], log_f, sta, keep,
                              H_A, REC_CHUNK, masks, eye, tri, pump)
        y_a = _head_out(oa, hn_ref[...], zr[:, 3 * d_a:4 * d_a])
        pump(2.0)

        bx = zr[:, o_b:o_b + d_b]
        cw = cw_ref[...]
        rgp = rgp_ref[...]
        ext = jnp.concatenate([cvs[...] * keep, bx], axis=0)
        conv = rgp[0:1] + cw[3:4] * bx
        for jj in range(CONV_W - 1):
            off = SUBLANES - (CONV_W - 1) + jj
            conv = conv + cw[jj:jj + 1] * ext[off:off + n]
        cvs[...] = bx[n - SUBLANES:n]
        pump(2.0)
        a, u = _rglru_gates(conv, rwax_ref, rgp[1:2], rgp[2:3], rgp[3:4])
        pump(2.0)
        hseq, hlast = _linear_scan(a, u, hrs[0:1] * keep, pump)
        hrs[...] = jnp.broadcast_to(hlast, hrs.shape)
        y_b = _gelu_tanh(zr[:, o_b + d_b:o_c]) * hseq

        log_fc = _gla_log_f(zr[:, o_t:o_t + LANES], wal_ref, bal_ref[...])
        oc = _recurrence_tile(zr[:, o_c:o_c + dk_tot] * (dk_c ** -0.5),
                              zr[:, o_c + dk_tot:o_c + 2 * dk_tot],
                              zr[:, o_c + 2 * dk_tot:o_c + 2 * dk_tot + d_c], log_fc, stc, keep,
                              H_C, REC_CHUNK, masks, eye, tri, pump)
        y_c = _head_out(oc, gn_ref[...], zr[:, o_c + 2 * dk_tot + d_c:o_t])
        pump(len(todo))

        gates = _rot_gates(zr[:, o_t:o_end + LANES], d)
        o_ref[...] = _merge(x, gt_ref[0], gates, y_a, y_b, y_c, wbr_ref, wout_ref)

    @pl.when(lax.rem(s, 2) == 0)
    def _():
        step(z0, z1)

    @pl.when(lax.rem(s, 2) == 1)
    def _():
        step(z1, z0)

    @pl.when(jnp.logical_and(s >= 1, lax.rem(prev, tpb) == tpb - 1))
    def _():
        for hh in range(H_A):
            sa_ref[hh] = sta[hh].T
        for hp in range(H_C // 2):
            pair = jnp.concatenate([stc[2 * hp], stc[2 * hp + 1]], axis=1).T
            sg_ref[2 * hp] = pair[0:dk_c]
            sg_ref[2 * hp + 1] = pair[dk_c:2 * dk_c]
        cv_ref[...] = cvs[SUBLANES - (CONV_W - 1):SUBLANES]
        hr_ref[...] = hrs[0:1]


def _mixer_call(x, mod4, gain, win, lbp, hnorm, cw, rgp, rwax, wal, bal, gnorm, wbr, wout,
                layer, batch, seq, dims):
    n, d = x.shape
    d_a, d_b, dk_tot, d_c = dims
    tile = min(MIX_TILE, seq)
    assert seq % tile == 0 and tile % REC_CHUNK == 0
    tpb = seq // tile
    dk_a, dv_a = d_a // H_A, d_a // H_A
    dk_c, dv_c = dk_tot // H_C, d_c // H_C
    nmain = 4 * d_a + 2 * d_b + 2 * dk_tot + 2 * d_c + 3 * d
    nin = nmain + LANES
    assert win.shape[2] == nin and nmain % (2 * LANES) == 0
    nt = batch * tpb
    cur = lambda s: jnp.minimum(s, nt - 1)
    prv = lambda s: jnp.maximum(s - 1, 0)
    mspec = lambda col, t: pl.BlockSpec((None, 1, 1, d), lambda s: (layer, t(s) // tpb, 0, col))
    lsel = lambda *rest: (lambda s: (layer,) + rest)
    in_specs = [
        pl.BlockSpec((tile, d), lambda s: (cur(s), 0)),
        pl.BlockSpec((tile, d), lambda s: (prv(s), 0)),
        mspec(3, cur), mspec(4, cur), mspec(5, prv),
        pl.BlockSpec((None, None, 1, d), lsel(1, 0, 0)),
        pl.BlockSpec((None, d, nmain), lsel(0, 0), pipeline_mode=pl.Buffered(1)),
        pl.BlockSpec((None, d, LANES), lsel(0, nmain // LANES)),
        pl.BlockSpec((None, 3, d_a), lsel(0, 0)),
        pl.BlockSpec((None, 1, dv_a), lsel(0, 0)),
        pl.BlockSpec((None, CONV_W, d_b), lsel(0, 0)),
        pl.BlockSpec((None, 4, d_b), lsel(0, 0)),
        pl.BlockSpec((None, NB_B, d_b // NB_B, 2 * d_b // NB_B), lsel(0, 0, 0)),
        pl.BlockSpec((None, LANES, dk_tot), lsel(0, 0)),
        pl.BlockSpec((None, 1, dk_tot), lsel(0, 0)),
        pl.BlockSpec((None, 1, dv_c), lsel(0, 0)),
        pl.BlockSpec((None, 3, d_a, d), lsel(0, 0, 0), pipeline_mode=pl.Buffered(1)),
        pl.BlockSpec((None, d, d), lsel(0, 0), pipeline_mode=pl.Buffered(1)),
    ]
    out_specs = [
        pl.BlockSpec((tile, d), lambda s: (prv(s), 0)),
        pl.BlockSpec((None, H_A, dk_a, dv_a), lambda s: (prv(s) // tpb, 0, 0, 0)),
        pl.BlockSpec((None, CONV_W - 1, d_b), lambda s: (prv(s) // tpb, 0, 0)),
        pl.BlockSpec((None, 1, d_b), lambda s: (prv(s) // tpb, 0, 0)),
        pl.BlockSpec((None, H_C, dk_c, dv_c), lambda s: (prv(s) // tpb, 0, 0, 0)),
    ]
    out_shape = [
        jax.ShapeDtypeStruct((n, d), F32),
        jax.ShapeDtypeStruct((batch, H_A, dk_a, dv_a), F32),
        jax.ShapeDtypeStruct((batch, CONV_W - 1, d_b), F32),
        jax.ShapeDtypeStruct((batch, 1, d_b), F32),
        jax.ShapeDtypeStruct((batch, H_C, dk_c, dv_c), F32),
    ]
    scratch = [
        pltpu.VMEM((tile, nin), F32),
        pltpu.VMEM((tile, nin), F32),
        pltpu.VMEM((H_A, dv_a, dk_a), F32),
        pltpu.VMEM((H_C, dv_c, dk_c), F32),
        pltpu.VMEM((SUBLANES, d_b), F32),
        pltpu.VMEM((SUBLANES, d_b), F32),
    ]
    return pl.pallas_call(
        functools.partial(_mixer_kernel, dims=dims, tpb=tpb),
        grid=(nt + 1,),
        in_specs=in_specs,
        out_specs=out_specs,
        out_shape=out_shape,
        scratch_shapes=scratch,
        compiler_params=pltpu.CompilerParams(
            dimension_semantics=("arbitrary",), vmem_limit_bytes=VMEM_LIMIT_BYTES),
        name="mixer_prompt",
    )(x, x, mod4, mod4, mod4, gain, win, win, lbp, hnorm, cw, rgp, rwax, wal, bal,
      gnorm, wbr, wout)


def _dec_pre_kernel(x_ref, sh_ref, sc_ref, gain_ref, win_ref, wtl_ref, lbp_ref, cw_ref, rgp_ref,
                    rwax_ref, wal_ref, bal_ref, cvin_ref, hrin_ref,
                    sv_ref, post_ref, cvout_ref, hrout_ref, *, dims):
    d_a, d_b, dk_tot, d_c = dims
    x = x_ref[...]
    d = x.shape[1]
    h = _ada_norm(x, gain_ref[...], sh_ref[0], sc_ref[0]).astype(BF16)
    o_b = 4 * d_a
    o_c = o_b + 2 * d_b
    o_t = o_c + 2 * dk_tot + 2 * d_c
    o_end = o_t + 3 * d

    za = _dot(h, win_ref[:, 0:o_b])
    lbp = lbp_ref[...]
    q, k, log_f = _hgrn_gates(za[:, 0:d_a], za[:, d_a:2 * d_a], lbp[0:1], lbp[1:2], lbp[2:3])

    zb = _dot(h, win_ref[:, o_b:o_c])
    bx = zb[:, 0:d_b]
    cw = cw_ref[...]
    rgp = rgp_ref[...]
    cvin = cvin_ref[...]
    conv = rgp[0:1] + cw[3:4] * bx
    for jj in range(CONV_W - 1):
        conv = conv + cw[jj:jj + 1] * cvin[:, jj * d_b:(jj + 1) * d_b]
    a, u = _rglru_gates(conv, rwax_ref, rgp[1:2], rgp[2:3], rgp[3:4])
    hnew = u + a * hrin_ref[...]
    y_b = _gelu_tanh(zb[:, d_b:2 * d_b]) * hnew
    cvout_ref[...] = jnp.concatenate([cvin[:, d_b:(CONV_W - 1) * d_b], bx], axis=-1)
    hrout_ref[...] = hnew

    zc = _dot(h, win_ref[:, o_c:o_t])
    zt = jnp.concatenate([_dot(h, win_ref[:, o_t:o_end]), _dot(h, wtl_ref[...])], axis=-1)
    log_fc = _gla_log_f(zt[:, 0:LANES], wal_ref, bal_ref[...])
    dk_c = dk_tot // H_C

    sv_ref[...] = jnp.concatenate(
        [q, jnp.exp(log_f), k, za[:, 2 * d_a:3 * d_a],
         zc[:, 0:dk_tot] * (dk_c ** -0.5), jnp.exp(log_fc), zc[:, dk_tot:2 * dk_tot],
         zc[:, 2 * dk_tot:2 * dk_tot + d_c]], axis=-1)
    post_ref[...] = jnp.concatenate(
        [za[:, 3 * d_a:4 * d_a], y_b, zc[:, 2 * dk_tot + d_c:2 * dk_tot + 2 * d_c]]
        + _rot_gates(zt, d), axis=-1)


def _dec_pre_call(x, mod4, gain, win, lbp, cw, rgp, rwax, wal, bal, cv_all, hr_all, layer, dims):
    n, d = x.shape
    d_a, d_b, dk_tot, d_c = dims
    nmain = win.shape[2] - LANES
    sv_w = 4 * d_a + 3 * dk_tot + d_c
    post_w = d_a + d_b + d_c + 3 * d
    mspec = lambda col: pl.BlockSpec((None, 1, n, d), lambda i: (layer, 0, 0, col))
    lsel = lambda *rest: (lambda i: (layer,) + rest)
    in_specs = [
        pl.BlockSpec((n, d), lambda i: (0, 0)),
        mspec(3), mspec(4),
        pl.BlockSpec((None, None, 1, d), lsel(1, 0, 0)),
        pl.BlockSpec((None, d, nmain), lsel(0, 0), pipeline_mode=pl.Buffered(1)),
        pl.BlockSpec((None, d, LANES), lsel(0, nmain // LANES)),
        pl.BlockSpec((None, 3, d_a), lsel(0, 0)),
        pl.BlockSpec((None, CONV_W, d_b), lsel(0, 0)),
        pl.BlockSpec((None, 4, d_b), lsel(0, 0)),
        pl.BlockSpec((None, NB_B, d_b // NB_B, 2 * d_b // NB_B), lsel(0, 0, 0)),
        pl.BlockSpec((None, LANES, dk_tot), lsel(0, 0)),
        pl.BlockSpec((None, 1, dk_tot), lsel(0, 0)),
        pl.BlockSpec((None, n, (CONV_W - 1) * d_b), lsel(0, 0)),
        pl.BlockSpec((None, n, d_b), lsel(0, 0)),
    ]
    full = lambda w: pl.BlockSpec((n, w), lambda i: (0, 0))
    return pl.pallas_call(
        functools.partial(_dec_pre_kernel, dims=dims),
        grid=(1,),
        in_specs=in_specs,
        out_specs=[full(sv_w), full(post_w), full((CONV_W - 1) * d_b), full(d_b)],
        out_shape=[jax.ShapeDtypeStruct((n, sv_w), F32), jax.ShapeDtypeStruct((n, post_w), F32),
                   jax.ShapeDtypeStruct((n, (CONV_W - 1) * d_b), F32),
                   jax.ShapeDtypeStruct((n, d_b), F32)],
        compiler_params=pltpu.CompilerParams(
            dimension_semantics=("arbitrary",), vmem_limit_bytes=VMEM_LIMIT_BYTES),
        name="mixer_sample_pre",
    )(x, mod4, mod4, gain, win, win, lbp, cw, rgp, rwax, wal, bal, cv_all, hr_all)


def _columns(rows):
    pad = jnp.zeros((LANES - rows.shape[0], LANES), F32)
    return jnp.concatenate([rows, pad], axis=0).T


def _dec_state_kernel(sv_ref, sa_ref, sg_ref, *rest, dims):
    oa_ref, oc_ref, sa_out, sg_out = rest[-4:]
    d_a, d_b, dk_tot, d_c = dims
    sv = sv_ref[...]
    g = sv.shape[0]
    dk_a = d_a // H_A
    dk_c = dk_tot // H_C
    dv_c = d_c // H_C
    o_q, o_f, o_k, o_v = 0, d_a, 2 * d_a, 3 * d_a
    for hh in range(H_A):
        qt = _columns(sv[:, o_q + hh * dk_a:o_q + (hh + 1) * dk_a])
        ft = _columns(sv[:, o_f + hh * dk_a:o_f + (hh + 1) * dk_a])
        kt = _columns(sv[:, o_k + hh * dk_a:o_k + (hh + 1) * dk_a])
        for jj in range(g):
            v = sv[jj:jj + 1, o_v + hh * dk_a:o_v + (hh + 1) * dk_a]
            s_new = ft[:, jj:jj + 1] * sa_ref[jj, hh] + kt[:, jj:jj + 1] * v
            sa_out[jj, hh] = s_new
            oa_ref[jj:jj + 1, hh * dk_a:(hh + 1) * dk_a] = jnp.sum(
                qt[:, jj:jj + 1] * s_new, axis=0, keepdims=True)
    c_q = 4 * d_a
    c_f = c_q + dk_tot
    c_k = c_f + dk_tot
    c_v = c_k + dk_tot
    per_tile = LANES // dk_c
    for tp in range(dk_tot // LANES):
        qt = _columns(sv[:, c_q + tp * LANES:c_q + (tp + 1) * LANES])
        ft = _columns(sv[:, c_f + tp * LANES:c_f + (tp + 1) * LANES])
        kt = _columns(sv[:, c_k + tp * LANES:c_k + (tp + 1) * LANES])
        for hp in range(per_tile):
            hh = tp * per_tile + hp
            rs = slice(hp * dk_c, (hp + 1) * dk_c)
            for jj in range(g):
                v = sv[jj:jj + 1, c_v + hh * dv_c:c_v + (hh + 1) * dv_c]
                s_new = ft[rs, jj:jj + 1] * sg_ref[jj, hh] + kt[rs, jj:jj + 1] * v
                sg_out[jj, hh] = s_new
                oc_ref[jj:jj + 1, hh * dv_c:(hh + 1) * dv_c] = jnp.sum(
                    qt[rs, jj:jj + 1] * s_new, axis=0, keepdims=True)


def _dec_state_call(sv, st_a, st_c, layer, dims, stacked):
    n, sv_w = sv.shape
    d_a, d_b, dk_tot, d_c = dims
    depth = st_a.shape[0]
    g = DEC_GROUP
    assert n % g == 0
    dk_a = d_a // H_A
    dk_c, dv_c = dk_tot // H_C, d_c // H_C
    in_specs = [
        pl.BlockSpec((g, sv_w), lambda i: (i, 0)),
        pl.BlockSpec((None, g, H_A, dk_a, dk_a), lambda i: (layer, i, 0, 0, 0)),
        pl.BlockSpec((None, g, H_C, dk_c, dv_c), lambda i: (layer, i, 0, 0, 0)),
    ]
    in_specs += [pl.BlockSpec(memory_space=pl.ANY), pl.BlockSpec(memory_space=pl.ANY)]
    args = [sv, st_a, st_c, *stacked]
    aliases = {3: 2, 4: 3}
    return pl.pallas_call(
        functools.partial(_dec_state_kernel, dims=dims),
        grid=(n // g,),
        in_specs=in_specs,
        out_specs=[
            pl.BlockSpec((g, d_a), lambda i: (i, 0)),
            pl.BlockSpec((g, d_c), lambda i: (i, 0)),
            pl.BlockSpec((None, g, H_A, dk_a, dk_a), lambda i: (layer, i, 0, 0, 0)),
            pl.BlockSpec((None, g, H_C, dk_c, dv_c), lambda i: (layer, i, 0, 0, 0)),
        ],
        out_shape=[
            jax.ShapeDtypeStruct((n, d_a), F32),
            jax.ShapeDtypeStruct((n, d_c), F32),
            jax.ShapeDtypeStruct((depth, n, H_A, dk_a, dk_a), F32),
            jax.ShapeDtypeStruct((depth, n, H_C, dk_c, dv_c), F32),
        ],
        input_output_aliases=aliases,
        compiler_params=pltpu.CompilerParams(
            dimension_semantics=("arbitrary",), vmem_limit_bytes=VMEM_LIMIT_BYTES),
        name="mixer_sample_state",
    )(*args)


def _dec_post_kernel(x_ref, gt_ref, post_ref, oa_ref, oc_ref, hn_ref, gn_ref, wbr_ref, wout_ref,
                     o_ref, *, dims):
    d_a, d_b, dk_tot, d_c = dims
    x = x_ref[...]
    post = post_ref[...]
    oa = oa_ref[...]
    oc = oc_ref[...]
    dv_a = d_a // H_A
    dv_c = d_c // H_C
    y_a = _head_out([oa[:, hh * dv_a:(hh + 1) * dv_a] for hh in range(H_A)], hn_ref[...],
                    post[:, 0:d_a])
    y_b = post[:, d_a:d_a + d_b]
    y_c = _head_out([oc[:, hh * dv_c:(hh + 1) * dv_c] for hh in range(H_C)], gn_ref[...],
                    post[:, d_a + d_b:d_a + d_b + d_c])
    off = d_a + d_b + d_c
    d = x.shape[1]
    gates = [post[:, off + i * d:off + (i + 1) * d] for i in range(3)]
    o_ref[...] = _merge(x, gt_ref[0], gates, y_a, y_b, y_c, wbr_ref, wout_ref)


def _dec_post_call(x, mod4, post, oa, oc, hnorm, gnorm, wbr, wout, layer, dims):
    n, d = x.shape
    d_a, d_b, dk_tot, d_c = dims
    lsel = lambda *rest: (lambda i: (layer,) + rest)
    full = lambda w: pl.BlockSpec((n, w), lambda i: (0, 0))
    return pl.pallas_call(
        functools.partial(_dec_post_kernel, dims=dims),
        grid=(1,),
        in_specs=[
            full(d),
            pl.BlockSpec((None, 1, n, d), lambda i: (layer, 0, 0, 5)),
            full(post.shape[1]), full(d_a), full(d_c),
            pl.BlockSpec((None, 1, d_a // H_A), lsel(0, 0)),
            pl.BlockSpec((None, 1, d_c // H_C), lsel(0, 0)),
            pl.BlockSpec((None, 3, d_a, d), lsel(0, 0, 0), pipeline_mode=pl.Buffered(1)),
            pl.BlockSpec((None, d, d), lsel(0, 0), pipeline_mode=pl.Buffered(1)),
        ],
        out_specs=full(d),
        out_shape=jax.ShapeDtypeStruct((n, d), F32),
        compiler_params=pltpu.CompilerParams(
            dimension_semantics=("arbitrary",), vmem_limit_bytes=VMEM_LIMIT_BYTES),
        name="mixer_sample_post",
    )(x, mod4, post, oa, oc, hnorm, gnorm, wbr, wout)


def kernel(x_prompt, x_sample, c_prompt, c_sample, state_hgrn, state_conv, state_rglru, state_gla,
           w_ada, b_ada, norm_gain, w_ffn_in, w_ffn_out, w_in, hgrn_lb_logits, hgrn_norm,
           conv_w, conv_b, rg_wa, rg_ba, rg_wx, rg_bx, rg_lambda, gla_w_alpha, gla_b_alpha,
           gla_norm, w_br_a, w_br_b, w_br_c, w_out, final_norm):
    batch, seq, d = x_prompt.shape
    nd = x_sample.shape[0]
    depth = w_in.shape[0]
    d_a = hgrn_lb_logits.shape[1]
    d_b = conv_b.shape[1]
    dk_tot = gla_b_alpha.shape[1]
    d_c = w_br_c.shape[1]
    dims = (d_a, d_b, dk_tot, d_c)
    nm = w_ada.shape[2]

    wi = w_ffn_in.astype(BF16)
    wo = w_ffn_out.astype(BF16)
    o_ca = 4 * d_a + 2 * d_b + 2 * dk_tot + 2 * d_c
    assert w_in.shape[2] == o_ca + 3 * d + GATE_RANK
    win = jnp.pad(w_in, ((0, 0), (0, 0), (0, LANES - GATE_RANK))).astype(BF16)
    wal = jnp.concatenate(
        [gla_w_alpha, jnp.zeros((depth, LANES - GATE_RANK, dk_tot), gla_w_alpha.dtype)],
        axis=1).astype(BF16)
    bal = gla_b_alpha.reshape(depth, 1, dk_tot)
    wbr = jnp.roll(jnp.stack([w_br_a, w_br_b, w_br_c], axis=1), GATE_RANK, axis=-1).astype(BF16)
    wout = jnp.roll(w_out, GATE_RANK, axis=1).astype(BF16)
    rwax = jnp.concatenate([rg_wa, rg_wx], axis=-1).astype(BF16)
    rgp = jnp.stack([conv_b, rg_ba, rg_bx, rg_lambda], axis=1)
    hnorm = hgrn_norm.reshape(depth, 1, -1)
    gnorm = gla_norm.reshape(depth, 1, -1)
    lb = jnp.cumsum(jax.nn.softmax(hgrn_lb_logits.astype(F32), axis=0), axis=0)
    lb = lb - lb[0]
    lbp = jnp.stack([jnp.log(lb), jnp.log1p(-lb), 1.0 - lb], axis=1)

    norm_gain = norm_gain.reshape(depth, N_SUB, 1, d)
    c_all = jnp.concatenate([c_sample, c_prompt], axis=0)
    mod = _ada_call(c_all, w_ada, b_ada)
    mod_p = mod[:, nd:].reshape(depth, batch, 1, nm)
    mod_s = mod.reshape(depth, 1, nd + batch, nm)

    ffn_tile = min(FFN_TILE, seq)
    tpg = seq // ffn_tile
    x = x_prompt.reshape(batch * seq, d)
    sa_p, cv_p, hr_p, sg_p = [], [], [], []
    zero_a = zero_c = None
    for l in range(depth):
        x = _ffn_call(x, mod_p, norm_gain, wi, wo, l, 0, 0, ffn_tile, tpg,
                      fill_shape=state_hgrn.shape if l == 0 else None)
        if l == 0:
            x, zero_a = x
        x, sa, cv, hr, sg = _mixer_call(x, mod_p, norm_gain, win, lbp, hnorm, conv_w, rgp, rwax,
                                        wal, bal, gnorm, wbr, wout, l, batch, seq, dims)
        x = _ffn_call(x, mod_p, norm_gain, wi, wo, l, 1, 2, ffn_tile, tpg,
                      final_gain=final_norm if l == depth - 1 else None,
                      fill_shape=state_gla.shape if l == 0 else None)
        if l == 0:
            x, zero_c = x
        sa_p.append(sa); cv_p.append(cv); hr_p.append(hr[:, 0]); sg_p.append(sg)
    y_prompt = x.reshape(batch, seq, d)

    xs = x_sample.reshape(nd, d)
    cv_all = state_conv.reshape(depth, nd, (CONV_W - 1) * d_b)
    cv_s, hr_s = [], []
    stacked = (zero_a, zero_c)
    for l in range(depth):
        xs = _ffn_call(xs, mod_s, norm_gain, wi, wo, l, 0, 0, nd, 1)
        sv, post, cv, hr = _dec_pre_call(xs, mod_s, norm_gain, win, lbp, conv_w, rgp, rwax,
                                         wal, bal, cv_all, state_rglru, l, dims)
        oa, oc, sa_s, sg_s = _dec_state_call(sv, state_hgrn, state_gla, l, dims, stacked)
        stacked = (sa_s, sg_s)
        xs = _dec_post_call(xs, mod_s, post, oa, oc, hnorm, gnorm, wbr, wout, l, dims)
        xs = _ffn_call(xs, mod_s, norm_gain, wi, wo, l, 1, 2, nd, 1,
                       final_gain=final_norm if l == depth - 1 else None)
        cv_s.append(cv.reshape(nd, CONV_W - 1, d_b)); hr_s.append(hr)
    y_sample = xs.reshape(nd, 1, d)

    st = jnp.stack
    return (y_prompt, y_sample, st(sa_p), sa_s, st(cv_p), st(cv_s), st(hr_p), st(hr_s),
            st(sg_p), sg_s)
```
